```python
import numpy as np
import jax
import jax.numpy as jnp
from jax import lax

D_MODEL = 4096
BATCH = 1
SEQ = 16384
DEPTH = 4

HEAD_DIM = 128
H_A = 12
H_B = 12
H_M = 4
D_M = 256
N_MEM = 256
W_A = H_A * HEAD_DIM
W_B = H_B * HEAD_DIM
W_M = H_M * D_M
MIX_WIDTH = W_A + W_B + W_M

Q_LORA = 1024
KV_LORA = 512
D_NOPE = HEAD_DIM
D_ROPE = 64
H_IDX = 32
D_IDX = 64
D_IDX_ROPE = 32
TOPK_MAX = 256

G_B = 4
HPG_B = H_B // G_B
CMP_LEN = 32
CMP_STRIDE = 16
CMP_HID = 256
SLC_BLOCK = 64
N_SLC = 16
WINDOW = 512

IN_SPLITS = (Q_LORA, KV_LORA, D_ROPE, D_IDX, H_IDX, W_B, 3 * 2 * G_B * HEAD_DIM, 3 * H_B, W_M)
N_IN = Q_LORA + KV_LORA + D_ROPE + D_IDX + H_IDX + W_B + 3 * 2 * G_B * HEAD_DIM + 3 * H_B + W_M

D_FF = 11008
N_EXP = 8
TOP_E = 2
D_FF_E = 1792

Q_BLK = 128
ROPE_THETA = 10000.0
EPS = 1e-5
ALPHA = (2 * DEPTH) ** 0.25
BETA = (8 * DEPTH) ** -0.25

kernel_name = 'hybrid_dsa_nsa_memory_deepnorm_moe'


def layer_norm(x, g, b):
    xf = x.astype(jnp.float32)
    mu = jnp.mean(xf, -1, keepdims=True)
    var = jnp.mean(jnp.square(xf - mu), -1, keepdims=True)
    return ((xf - mu) * lax.rsqrt(var + EPS) * g + b).astype(x.dtype)


def rms_norm(x, g):
    xf = x.astype(jnp.float32)
    return (xf * lax.rsqrt(jnp.mean(jnp.square(xf), -1, keepdims=True) + EPS) * g).astype(x.dtype)


def rope(t, pos):
    d = t.shape[-1]
    inv_freq = jnp.power(ROPE_THETA, -jnp.arange(0, d, 2, dtype=jnp.float32) / d)
    ang = pos.astype(jnp.float32)[..., None] * inv_freq
    ang = ang.reshape(ang.shape[:2] + (1,) * (t.ndim - 3) + (d // 2,))
    cos, sin = jnp.cos(ang), jnp.sin(ang)
    tf = t.astype(jnp.float32)
    t1, t2 = tf[..., : d // 2], tf[..., d // 2:]
    return jnp.concatenate([t1 * cos - t2 * sin, t2 * cos + t1 * sin], -1).astype(t.dtype)


def partial_rope(t, pos):
    return jnp.concatenate([rope(t[..., :D_IDX_ROPE], pos), t[..., D_IDX_ROPE:]], -1)


def masked_softmax(s, mask):
    s = jnp.where(mask, s, -jnp.inf)
    m = jnp.max(s, -1, keepdims=True)
    m = jnp.where(jnp.isfinite(m), m, 0.0)
    e = jnp.where(mask, jnp.exp(s - m), 0.0)
    return e / jnp.maximum(jnp.sum(e, -1, keepdims=True), 1e-30)


def dsa_mixer(c_q, c_kv, k_r, k_idx, w_idx, pos, g_cq, g_ckv, w_uq, w_iq, w_uk, w_uv):
    B, S, _ = c_q.shape
    topk = min(TOPK_MAX, S // 4)
    c_q = rms_norm(c_q, g_cq)
    c_kv = rms_norm(c_kv, g_ckv)
    q = jnp.einsum('bsr,rf->bsf', c_q, w_uq).reshape(B, S, H_A, D_NOPE + D_ROPE)
    q_nope = q[..., :D_NOPE]
    q_pe = rope(q[..., D_NOPE:], pos)
    k_pe = rope(k_r, pos)
    q_i = partial_rope(jnp.einsum('bsr,rf->bsf', c_q, w_iq).reshape(B, S, H_IDX, D_IDX), pos)
    k_i = partial_rope(k_idx, pos)
    w_i = w_idx.astype(jnp.float32) * (H_IDX ** -0.5 * D_IDX ** -0.5)
    scale = (D_NOPE + D_ROPE) ** -0.5
    key_pos = jnp.arange(S)
    gather_rows = jax.vmap(lambda src, idx: src[idx])

    def block(t0):
        tq = t0 + jnp.arange(Q_BLK)
        qn = lax.dynamic_slice_in_dim(q_nope, t0, Q_BLK, axis=1)
        qp = lax.dynamic_slice_in_dim(q_pe, t0, Q_BLK, axis=1)
        qi = lax.dynamic_slice_in_dim(q_i, t0, Q_BLK, axis=1)
        wi = lax.dynamic_slice_in_dim(w_i, t0, Q_BLK, axis=1)
        logits = jnp.einsum('bthd,bsd->bths', qi, k_i, preferred_element_type=jnp.float32)
        score = jnp.einsum('bth,bths->bts', wi, jax.nn.relu(logits))
        causal = key_pos[None, :] <= tq[:, None]
        score = jnp.where(causal[None], score, -jnp.inf)
        top_val, top_idx = lax.top_k(score, topk)
        valid = jnp.isfinite(top_val)
        ckv_sel = gather_rows(c_kv, top_idx)
        kpe_sel = gather_rows(k_pe, top_idx)
        q_lat = jnp.einsum('bthd,chd->bthc', qn, w_uk)
        s = (jnp.einsum('bthc,btkc->bthk', q_lat, ckv_sel, preferred_element_type=jnp.float32)
             + jnp.einsum('bthr,btkr->bthk', qp, kpe_sel, preferred_element_type=jnp.float32)) * scale
        p = masked_softmax(s, valid[:, :, None, :])
        o_lat = jnp.einsum('bthk,btkc->bthc', p.astype(ckv_sel.dtype), ckv_sel)
        return jnp.einsum('bthc,chd->bthd', o_lat, w_uv)

    starts = jnp.arange(S // Q_BLK, dtype=jnp.int32) * Q_BLK
    out = lax.map(block, starts)
    return jnp.transpose(out, (1, 0, 2, 3, 4)).reshape(B, S, W_A)


def nsa_mixer(q, kv, gate_logits, pos, w1, b1, w2, b2, pe):
    B, S, _ = q.shape
    q = rope(q.reshape(B, S, H_B, HEAD_DIM), pos)
    kv = kv.reshape(B, S, 3, 2, G_B, HEAD_DIM)
    k_c, v_c = rope(kv[:, :, 0, 0], pos), kv[:, :, 0, 1]
    k_s, v_s = rope(kv[:, :, 1, 0], pos), kv[:, :, 1, 1]
    k_w, v_w = rope(kv[:, :, 2, 0], pos), kv[:, :, 2, 1]
    gates = jax.nn.sigmoid(gate_logits.astype(jnp.float32)).reshape(B, S, 3, G_B, HPG_B)
    scale = HEAD_DIM ** -0.5

    n_cmp = (S - CMP_LEN) // CMP_STRIDE + 1
    blk_tok = np.arange(n_cmp)[:, None] * CMP_STRIDE + np.arange(CMP_LEN)[None, :]

    def compress(t, j):
        tb = t[:, blk_tok] + pe[j][None, None, :, None, :]
        hdn = jax.nn.gelu(jnp.einsum('bnlgd,ldf->bngf', tb, w1[j].reshape(CMP_LEN, HEAD_DIM, CMP_HID)) + b1[j])
        return jnp.einsum('bngf,fd->bngd', hdn, w2[j]) + b2[j]

    kc = compress(k_c, 0)
    vc = compress(v_c, 1)
    cmp_end = jnp.asarray(blk_tok[:, -1])

    n_sb = S // SLC_BLOCK
    n_sel = min(N_SLC, n_sb)
    cs = np.arange(n_cmp) * CMP_STRIDE
    ss = np.arange(n_sb) * SLC_BLOCK
    cmp_to_slc = jnp.asarray(((cs[:, None] < ss[None, :] + SLC_BLOCK) & (cs[:, None] + CMP_LEN > ss[None, :])).astype(np.float32))
    blk_ids = jnp.arange(n_sb)

    k_w_pad = jnp.pad(k_w, ((0, 0), (WINDOW, 0), (0, 0), (0, 0)))
    v_w_pad = jnp.pad(v_w, ((0, 0), (WINDOW, 0), (0, 0), (0, 0)))
    gather_groups = jax.vmap(jax.vmap(lambda src, idx: src[idx], in_axes=(1, 1), out_axes=1))

    def block(t0):
        tq = t0 + jnp.arange(Q_BLK)
        qb = lax.dynamic_slice_in_dim(q, t0, Q_BLK, axis=1).reshape(B, Q_BLK, G_B, HPG_B, HEAD_DIM)
        s_c = jnp.einsum('btghd,bngd->btghn', qb, kc, preferred_element_type=jnp.float32) * scale
        m_c = (cmp_end[None, :] <= tq[:, None])[None, :, None, None, :]
        p_c = masked_softmax(s_c, m_c)
        o_c = jnp.einsum('btghn,bngd->btghd', p_c.astype(vc.dtype), vc)
        imp = jnp.einsum('btghn,nj->btgj', p_c, cmp_to_slc)
        cur = (tq // SLC_BLOCK)[:, None]
        forced = (blk_ids[None] == 0) | (blk_ids[None] == cur) | (blk_ids[None] == cur - 1)
        causal_b = blk_ids[None] * SLC_BLOCK <= tq[:, None]
        imp = jnp.where(forced[None, :, None, :], jnp.inf, imp)
        imp = jnp.where(causal_b[None, :, None, :], imp, -jnp.inf)
        sel_val, sel_blk = lax.top_k(imp, n_sel)
        tok = sel_blk[..., None] * SLC_BLOCK + jnp.arange(SLC_BLOCK)
        tok_ok = (sel_val > -jnp.inf)[..., None] & (tok <= tq[None, :, None, None, None])
        tok = tok.reshape(B, Q_BLK, G_B, n_sel * SLC_BLOCK)
        tok_ok = tok_ok.reshape(B, Q_BLK, G_B, n_sel * SLC_BLOCK)
        ks = gather_groups(k_s, tok)
        vs = gather_groups(v_s, tok)
        s_s = jnp.einsum('btghd,btgld->btghl', qb, ks, preferred_element_type=jnp.float32) * scale
        p_s = masked_softmax(s_s, tok_ok[:, :, :, None, :])
        o_s = jnp.einsum('btghl,btgld->btghd', p_s.astype(vs.dtype), vs)
        kw = lax.dynamic_slice_in_dim(k_w_pad, t0, Q_BLK + WINDOW, axis=1)
        vw = lax.dynamic_slice_in_dim(v_w_pad, t0, Q_BLK + WINDOW, axis=1)
        kpos = t0 - WINDOW + jnp.arange(Q_BLK + WINDOW)
        m_w = (kpos[None] >= 0) & (kpos[None] <= tq[:, None]) & (kpos[None] > tq[:, None] - WINDOW)
        s_w = jnp.einsum('btghd,bjgd->btghj', qb, kw, preferred_element_type=jnp.float32) * scale
        p_w = masked_softmax(s_w, m_w[None, :, None, None, :])
        o_w = jnp.einsum('btghj,bjgd->btghd', p_w.astype(vw.dtype), vw)
        g = lax.dynamic_slice_in_dim(gates, t0, Q_BLK, axis=1)
        o = (g[:, :, 0, :, :, None] * o_c + g[:, :, 1, :, :, None] * o_s
             + g[:, :, 2, :, :, None] * o_w)
        return o.reshape(B, Q_BLK, W_B).astype(q.dtype)

    starts = jnp.arange(S // Q_BLK, dtype=jnp.int32) * Q_BLK
    out = lax.map(block, starts)
    return jnp.transpose(out, (1, 0, 2, 3)).reshape(B, S, W_B)


def memory_mixer(q_m, mem_kv):
    B, S, _ = q_m.shape
    n_mem = mem_kv.shape[1]
    q = q_m.reshape(B, S, H_M, D_M)
    kv = mem_kv.reshape(B, n_mem, 2, H_M, D_M)
    s = jnp.einsum('bshd,bmhd->bhsm', q, kv[:, :, 0], preferred_element_type=jnp.float32) * D_M ** -0.5
    p = jax.nn.softmax(s, axis=-1)
    o = jnp.einsum('bhsm,bmhd->bshd', p.astype(q.dtype), kv[:, :, 1])
    return o.reshape(B, S, W_M)


def swiglu(x, wg, wu, wd):
    return (jax.nn.silu(x @ wg) * (x @ wu)) @ wd


def moe_swiglu(x, router_w, wg, wu, wd):
    logits = jnp.einsum('bsd,de->bse', x, router_w, preferred_element_type=jnp.float32)
    top_val, top_idx = lax.top_k(logits, TOP_E)
    top_gate = jax.nn.softmax(top_val, axis=-1)
    gate = jnp.sum(jax.nn.one_hot(top_idx, N_EXP, dtype=jnp.float32) * top_gate[..., None], axis=-2)
    y = jnp.zeros(x.shape, jnp.float32)
    for e in range(N_EXP):
        y = y + gate[..., e:e + 1] * swiglu(x, wg[e], wu[e], wd[e]).astype(jnp.float32)
    return y.astype(x.dtype)


def setup_inputs(seed: int = 0) -> dict:
    key = jax.random.key(seed)
    ks = iter(jax.random.split(key, 32))
    L = DEPTH
    ND = (DEPTH + 1) // 2
    NM = DEPTH // 2

    def nrm(shape, scale):
        return jax.random.normal(next(ks), shape, jnp.float32) * scale

    x = nrm((BATCH, SEQ, D_MODEL), 1.0)
    mem = nrm((BATCH, N_MEM, D_MODEL), 1.0)
    start = jax.random.randint(next(ks), (BATCH, 1), 0, 4096, dtype=jnp.int32)
    positions = start + jnp.arange(SEQ, dtype=jnp.int32)[None, :]
    return {
        'x': x,
        'mem': mem,
        'positions': positions,
        'w_in': nrm((L, D_MODEL, N_IN), D_MODEL ** -0.5),
        'g_cq': 1.0 + nrm((L, Q_LORA), 0.02),
        'g_ckv': 1.0 + nrm((L, KV_LORA), 0.02),
        'w_uq': nrm((L, Q_LORA, H_A * (D_NOPE + D_ROPE)), Q_LORA ** -0.5),
        'w_iq': nrm((L, Q_LORA, H_IDX * D_IDX), Q_LORA ** -0.5),
        'w_uk': nrm((L, KV_LORA, H_A, D_NOPE), KV_LORA ** -0.5),
        'w_uv': nrm((L, KV_LORA, H_A, HEAD_DIM), KV_LORA ** -0.5),
        'cmp_w1': nrm((L, 2, CMP_LEN * HEAD_DIM, CMP_HID), (CMP_LEN * HEAD_DIM) ** -0.5),
        'cmp_b1': nrm((L, 2, CMP_HID), 0.02),
        'cmp_w2': nrm((L, 2, CMP_HID, HEAD_DIM), CMP_HID ** -0.5),
        'cmp_b2': nrm((L, 2, HEAD_DIM), 0.02),
        'cmp_pe': nrm((L, 2, CMP_LEN, HEAD_DIM), 0.02),
        'gate_b': nrm((L, 3 * H_B), 0.02),
        'w_mem_kv': nrm((L, D_MODEL, 2 * W_M), D_MODEL ** -0.5),
        'w_out': nrm((L, MIX_WIDTH, D_MODEL), MIX_WIDTH ** -0.5 * BETA),
        'ln1_g': 1.0 + nrm((L, D_MODEL), 0.02),
        'ln1_b': nrm((L, D_MODEL), 0.02),
        'ln2_g': 1.0 + nrm((L, D_MODEL), 0.02),
        'ln2_b': nrm((L, D_MODEL), 0.02),
        'ffn_wg': nrm((ND, D_MODEL, D_FF), D_MODEL ** -0.5),
        'ffn_wu': nrm((ND, D_MODEL, D_FF), D_MODEL ** -0.5),
        'ffn_wd': nrm((ND, D_FF, D_MODEL), D_FF ** -0.5 * BETA),
        'router_w': nrm((NM, D_MODEL, N_EXP), D_MODEL ** -0.5),
        'moe_wg': nrm((NM, N_EXP, D_MODEL, D_FF_E), D_MODEL ** -0.5),
        'moe_wu': nrm((NM, N_EXP, D_MODEL, D_FF_E), D_MODEL ** -0.5),
        'moe_wd': nrm((NM, N_EXP, D_FF_E, D_MODEL), D_FF_E ** -0.5 * BETA),
    }


def reference(x, mem, positions, w_in, g_cq, g_ckv, w_uq, w_iq, w_uk, w_uv,
              cmp_w1, cmp_b1, cmp_w2, cmp_b2, cmp_pe, gate_b, w_mem_kv, w_out,
              ln1_g, ln1_b, ln2_g, ln2_b, ffn_wg, ffn_wu, ffn_wd,
              router_w, moe_wg, moe_wu, moe_wd):
    cuts = np.cumsum(IN_SPLITS)[:-1].tolist()
    for i in range(DEPTH):
        proj = jnp.einsum('bsd,df->bsf', x, w_in[i])
        c_q, c_kv, k_r, k_idx, w_idx, q_b, kv_b, gate_l, q_m = jnp.split(proj, cuts, axis=-1)
        o_a = dsa_mixer(c_q, c_kv, k_r, k_idx, w_idx, positions,
                        g_cq[i], g_ckv[i], w_uq[i], w_iq[i], w_uk[i], w_uv[i])
        o_b = nsa_mixer(q_b, kv_b, gate_l + gate_b[i], positions,
                        cmp_w1[i], cmp_b1[i], cmp_w2[i], cmp_b2[i], cmp_pe[i])
        mem_kv = jnp.einsum('bmd,df->bmf', mem, w_mem_kv[i])
        o_m = memory_mixer(q_m, mem_kv)
        mix = jnp.einsum('bsf,fd->bsd', jnp.concatenate([o_a, o_b, o_m], axis=-1), w_out[i])
        x = layer_norm(ALPHA * x + mix, ln1_g[i], ln1_b[i])
        j = i // 2
        if i % 2 == 0:
            f = swiglu(x, ffn_wg[j], ffn_wu[j], ffn_wd[j])
        else:
            f = moe_swiglu(x, router_w[j], moe_wg[j], moe_wu[j], moe_wd[j])
        x = layer_norm(ALPHA * x + f, ln2_g[i], ln2_b[i])
    return x
```

```python
import functools
import math

import numpy as np
import jax
import jax.numpy as jnp
from jax import lax
from jax.experimental import pallas as pl
from jax.experimental.pallas import tpu as pltpu

F32 = jnp.float32
CDT = jnp.bfloat16

HEAD_DIM = 128
H_A = 12
H_B = 12
H_M = 4
D_M = 256
Q_LORA = 1024
KV_LORA = 512
D_NOPE = 128
D_ROPE = 64
H_IDX = 32
D_IDX = 64
D_IDX_ROPE = 32
TOPK_MAX = 256
G_B = 4
HPG_B = H_B // G_B
CMP_LEN = 32
CMP_STRIDE = 16
CMP_HID = 256
SLC_BLOCK = 64
N_SLC = 16
WINDOW = 512
TOP_E = 2
ROPE_THETA = 10000.0
EPS = 1e-5

W_A = H_A * HEAD_DIM
W_B = H_B * HEAD_DIM
W_M = H_M * D_M
N_KVB = 3 * 2 * G_B * HEAD_DIM
N_GATE = 3 * H_B
IN_SPLITS = (Q_LORA, KV_LORA, D_ROPE, D_IDX, H_IDX, W_B, N_KVB, N_GATE, W_M)

LANE = 128
NEG = -1e30
INT_MIN = -2147483648
VMEM_LIMIT = 48 * 1024 * 1024

TQ_IDX = 128
TK_IDX = 512
TQ_ATT = 256
TK_ATT = 512


def _cp(*sem):
    return pltpu.CompilerParams(dimension_semantics=sem, vmem_limit_bytes=VMEM_LIMIT)


def _pick(n, cands):
    for c in cands:
        if c <= n and n % c == 0:
            return c
    return n


def _mm_body(x_ref, w_ref, o_ref, *scratch, nk):
    part = jnp.dot(x_ref[...], w_ref[...], preferred_element_type=F32)
    if nk == 1:
        o_ref[...] = part.astype(o_ref.dtype)
        return
    acc_ref, = scratch
    k = pl.program_id(2)

    @pl.when(k == 0)
    def _():
        acc_ref[...] = part

    @pl.when(k > 0)
    def _():
        acc_ref[...] += part

    @pl.when(k == nk - 1)
    def _():
        o_ref[...] = acc_ref[...].astype(o_ref.dtype)


def _mm(x, w, out_dtype, name):
    M, K = x.shape
    N = w.shape[1]
    tm = _pick(M, (512, 256, 128))
    tn = _pick(N, (1024, 512, 256, 128))
    tk = K
    for parts in (1, 2, 4, 8, 16):
        if K % parts == 0 and (K // parts) % LANE == 0 and (K // parts) * max(tm, tn) * 2 <= 12 * 2 ** 20:
            tk = K // parts
            break
    nk = K // tk
    scratch = [pltpu.VMEM((tm, tn), F32)] if nk > 1 else []
    return pl.pallas_call(
        functools.partial(_mm_body, nk=nk),
        out_shape=jax.ShapeDtypeStruct((M, N), out_dtype),
        grid=(M // tm, N // tn, nk),
        in_specs=[pl.BlockSpec((tm, tk), lambda i, j, k: (i, k)),
                  pl.BlockSpec((tk, tn), lambda i, j, k: (k, j))],
        out_specs=pl.BlockSpec((tm, tn), lambda i, j, k: (i, j)),
        scratch_shapes=scratch,
        compiler_params=_cp("parallel", "parallel", "arbitrary"),
        name=name,
    )(x, w)


def _mm_rows_body(*refs, nk, n_aux, n_out, epi):
    x_ref, w_ref = refs[0], refs[1]
    aux = refs[2:2 + n_aux]
    outs = refs[2 + n_aux:2 + n_aux + n_out]
    acc_ref = refs[2 + n_aux + n_out]
    k = pl.program_id(1)
    N = w_ref.shape[1]
    cw = _pick(N, (512, 256, 128))
    x = x_ref[...]

    @pl.when(k == 0)
    def _():
        for c in range(0, N, cw):
            acc_ref[:, c:c + cw] = jnp.dot(x, w_ref[:, c:c + cw], preferred_element_type=F32)

    @pl.when(k > 0)
    def _():
        for c in range(0, N, cw):
            acc_ref[:, c:c + cw] += jnp.dot(x, w_ref[:, c:c + cw], preferred_element_type=F32)

    @pl.when(k == nk - 1)
    def _():
        epi(acc_ref, aux, outs)


def _mm_rows(x, w, epi, aux, outs, name, tm=None):
    M, K = x.shape
    N = w.shape[1]
    tm = tm or _pick(M, (512, 256, 128))
    tk = _pick(K, (512, 256, 128))
    nk = K // tk
    in_specs = [pl.BlockSpec((tm, tk), lambda i, k: (i, k)),
                pl.BlockSpec((tk, N), lambda i, k: (k, 0))]
    for a, kind in aux:
        if kind == "row":
            in_specs.append(pl.BlockSpec((tm, a.shape[1]), lambda i, k: (i, 0)))
        else:
            in_specs.append(pl.BlockSpec(a.shape, lambda i, k, nd=a.ndim: (0,) * nd))
    res = pl.pallas_call(
        functools.partial(_mm_rows_body, nk=nk, n_aux=len(aux), n_out=len(outs), epi=epi),
        out_shape=[jax.ShapeDtypeStruct((M, c), dt) for c, dt in outs],
        grid=(M // tm, nk),
        in_specs=in_specs,
        out_specs=[pl.BlockSpec((tm, c), lambda i, k: (i, 0)) for c, _ in outs],
        scratch_shapes=[pltpu.VMEM((tm, N), F32)],
        compiler_params=_cp("parallel", "arbitrary"),
        name=name,
    )(x, w, *[a for a, _ in aux])
    return res


def _epi_plain(acc_ref, aux, outs):
    outs[0][...] = acc_ref[...].astype(outs[0].dtype)


def _rope_tab_body(pos_ref, inv_ref, sgn_ref, o_ref):
    pos = pos_ref[...].astype(F32)
    nt = inv_ref.shape[0]
    for t in range(nt):
        ang = pos * inv_ref[t:t + 1, :]
        o_ref[2 * t] = jnp.cos(ang)
        o_ref[2 * t + 1] = jnp.sin(ang) * sgn_ref[t:t + 1, :]


def _inv_freq(d):
    return jnp.power(ROPE_THETA, -jnp.arange(0, d, 2, dtype=F32) / d)


def _rope_tables(positions):
    S = positions.shape[1]
    i32, i16, i64 = _inv_freq(D_ROPE), _inv_freq(D_IDX_ROPE), _inv_freq(HEAD_DIM)
    z32, z64 = jnp.zeros((32,), F32), jnp.zeros((64,), F32)
    inv = jnp.stack([
        jnp.concatenate([i32, i32, i16, i16, z32]),
        jnp.concatenate([i32, i32, z64]),
        jnp.concatenate([i16, i16, z32, i16, i16, z32]),
        jnp.concatenate([i64, i64]),
    ])
    sgn = jnp.ones((4, LANE), F32).at[3, :64].set(-1.0)
    tm = _pick(S, (512, 256, 128))
    return pl.pallas_call(
        _rope_tab_body,
        out_shape=jax.ShapeDtypeStruct((8, S, LANE), F32),
        grid=(S // tm,),
        in_specs=[pl.BlockSpec((tm, 1), lambda i: (i, 0)),
                  pl.BlockSpec((4, LANE), lambda i: (0, 0)),
                  pl.BlockSpec((4, LANE), lambda i: (0, 0))],
        out_specs=pl.BlockSpec((8, tm, LANE), lambda i: (0, i, 0)),
        compiler_params=_cp("parallel"),
        name="rope_tables",
    )(positions.reshape(S, 1), inv, sgn)


N_PA = Q_LORA + KV_LORA + 4 * LANE


def _rms(v, g):
    return v * lax.rsqrt(jnp.mean(v * v, axis=-1, keepdims=True) + EPS) * g


def _epi_dsa_prep(acc_ref, aux, outs):
    gq_ref, gkv_ref, gb_ref, tc_ref, ts_ref = aux
    cq_ref, ckv_ref, kk_ref, wi_ref, gate_ref = outs
    o = Q_LORA + KV_LORA
    cq_ref[...] = _rms(acc_ref[:, 0:Q_LORA], gq_ref[...]).astype(cq_ref.dtype)
    ckv_ref[...] = _rms(acc_ref[:, Q_LORA:o], gkv_ref[...]).astype(ckv_ref.dtype)
    kk = acc_ref[:, o:o + LANE] * tc_ref[...] + acc_ref[:, o + LANE:o + 2 * LANE] * ts_ref[...]
    kk_ref[...] = kk.astype(kk_ref.dtype)
    wi_ref[...] = acc_ref[:, o + 2 * LANE:o + 3 * LANE] * (H_IDX ** -0.5 * D_IDX ** -0.5)
    gate_ref[...] = jax.nn.sigmoid(acc_ref[:, o + 3 * LANE:o + 4 * LANE] + gb_ref[...])


def _epi_q_nsa_mem(acc_ref, aux, outs):
    tc_ref, ts_ref = aux
    qb_ref, qm_ref = outs
    tc, ts = tc_ref[...], ts_ref[...]
    for h in range(H_B):
        t = acc_ref[:, h * HEAD_DIM:(h + 1) * HEAD_DIM]
        r = (t * tc + pltpu.roll(t, HEAD_DIM // 2, 1) * ts) * (HEAD_DIM ** -0.5)
        qb_ref[:, h * HEAD_DIM:(h + 1) * HEAD_DIM] = r.astype(qb_ref.dtype)
    qm_ref[...] = (acc_ref[:, W_B:W_B + W_M] * (D_M ** -0.5)).astype(qm_ref.dtype)


def _epi_kv_nsa(acc_ref, aux, outs):
    tc_ref, ts_ref = aux
    o_ref, = outs
    tc, ts = tc_ref[...], ts_ref[...]
    for br in range(3):
        for kv in range(2):
            for g in range(G_B):
                c = ((br * 2 + kv) * G_B + g) * HEAD_DIM
                t = acc_ref[:, c:c + HEAD_DIM]
                if kv == 0:
                    t = t * tc + pltpu.roll(t, HEAD_DIM // 2, 1) * ts
                o_ref[:, c:c + HEAD_DIM] = t.astype(o_ref.dtype)


def _epi_dsa_q(acc_ref, aux, outs):
    tc_ref, ts_ref = aux
    o_ref, = outs
    tc, ts = tc_ref[...], ts_ref[...]
    sc = (D_NOPE + D_ROPE) ** -0.5
    W = 2 * LANE
    for h in range(H_A):
        o_ref[:, h * W:h * W + LANE] = (acc_ref[:, h * W:h * W + LANE] * sc).astype(o_ref.dtype)
        pe = acc_ref[:, h * W + LANE:(h + 1) * W] * tc + acc_ref[:, H_A * W + h * LANE:H_A * W + (h + 1) * LANE] * ts
        o_ref[:, h * W + LANE:(h + 1) * W] = (pe * sc).astype(o_ref.dtype)


def _epi_idx_q(acc_ref, aux, outs):
    tc_ref, ts_ref = aux
    o_ref, = outs
    tc, ts = tc_ref[...], ts_ref[...]
    n = H_IDX * D_IDX
    for c in range(n // LANE):
        r = acc_ref[:, c * LANE:(c + 1) * LANE] * tc + acc_ref[:, n + c * LANE:n + (c + 1) * LANE] * ts
        o_ref[:, c * LANE:(c + 1) * LANE] = r.astype(o_ref.dtype)


def _res_ln_body(x_ref, f_ref, g_ref, b_ref, o_ref, *, alpha):
    v = alpha * x_ref[...] + f_ref[...].astype(F32)
    mu = jnp.mean(v, axis=-1, keepdims=True)
    d = v - mu
    var = jnp.mean(d * d, axis=-1, keepdims=True)
    o_ref[...] = d * lax.rsqrt(var + EPS) * g_ref[...] + b_ref[...]


def _res_ln(x, f, g, b, alpha):
    S, D = x.shape
    tm = _pick(S, (256, 128))
    row = pl.BlockSpec((tm, D), lambda i: (i, 0))
    vec = pl.BlockSpec((1, D), lambda i: (0, 0))
    return pl.pallas_call(
        functools.partial(_res_ln_body, alpha=alpha),
        out_shape=jax.ShapeDtypeStruct((S, D), F32),
        grid=(S // tm,),
        in_specs=[row, row, vec, vec],
        out_specs=row,
        compiler_params=_cp("parallel"),
        name="res_ln",
    )(x, f, g.reshape(1, D), b.reshape(1, D))


def _swiglu_up_body(x_ref, wg_ref, wu_ref, *rest, scaled):
    if scaled:
        s_ref, o_ref = rest
    else:
        o_ref, = rest
    x = x_ref[...]
    g = jnp.dot(x, wg_ref[...], preferred_element_type=F32)
    u = jnp.dot(x, wu_ref[...], preferred_element_type=F32)
    h = g * jax.nn.sigmoid(g) * u
    if scaled:
        h = h * s_ref[...]
    o_ref[...] = h.astype(o_ref.dtype)


def _swiglu_up(x, wg, wu, name, scale=None, n_per_scale=None):
    M, K = x.shape
    N = wg.shape[1]
    tm = _pick(M, (512, 256, 128))
    if scale is None:
        tn = _pick(N, (512, 256, 128))
    else:
        tn = _pick(n_per_scale, (512, 256, 128))
    in_specs = [pl.BlockSpec((tm, K), lambda i, j: (i, 0)),
                pl.BlockSpec((K, tn), lambda i, j: (0, j)),
                pl.BlockSpec((K, tn), lambda i, j: (0, j))]
    args = [x, wg, wu]
    if scale is not None:
        per = n_per_scale // tn
        in_specs.append(pl.BlockSpec((None, tm, 1), lambda i, j: (j // per, i, 0)))
        args.append(scale)
    return pl.pallas_call(
        functools.partial(_swiglu_up_body, scaled=scale is not None),
        out_shape=jax.ShapeDtypeStruct((M, N), CDT),
        grid=(M // tm, N // tn),
        in_specs=in_specs,
        out_specs=pl.BlockSpec((tm, tn), lambda i, j: (i, j)),
        compiler_params=_cp("parallel", "parallel"),
        name=name,
    )(*args)


def _make_epi_router(n_exp):
    def epi(acc_ref, aux, outs):
        o_ref, = outs
        lg = acc_ref[...]
        lane = lax.broadcasted_iota(jnp.int32, lg.shape, 1)
        lg = jnp.where(lane < n_exp, lg, -jnp.inf)
        m1 = jnp.max(lg, axis=-1, keepdims=True)
        i1 = jnp.min(jnp.where(lg == m1, lane, LANE), axis=-1, keepdims=True)
        l2 = jnp.where(lane == i1, -jnp.inf, lg)
        m2 = jnp.max(l2, axis=-1, keepdims=True)
        i2 = jnp.min(jnp.where(l2 == m2, lane, LANE), axis=-1, keepdims=True)
        e2 = jnp.exp(m2 - m1)
        den = 1.0 + e2
        o_ref[...] = jnp.where(lane == i1, 1.0 / den, 0.0) + jnp.where(lane == i2, e2 / den, 0.0)
    return epi


def _mem_attn_body(q_ref, kt_ref, v_ref, o_ref):
    for h in range(H_M):
        q = q_ref[:, h * D_M:(h + 1) * D_M]
        s = jnp.dot(q, kt_ref[h], preferred_element_type=F32)
        m = jnp.max(s, axis=-1, keepdims=True)
        e = jnp.exp(s - m)
        p = e / jnp.sum(e, axis=-1, keepdims=True)
        o = jnp.dot(p.astype(v_ref.dtype), v_ref[h], preferred_element_type=F32)
        o_ref[:, h * D_M:(h + 1) * D_M] = o.astype(o_ref.dtype)


def _mem_attn(q_m, k_t, v):
    S = q_m.shape[0]
    n_mem = v.shape[1]
    tm = _pick(S, (512, 256, 128))
    return pl.pallas_call(
        _mem_attn_body,
        out_shape=jax.ShapeDtypeStruct((S, W_M), CDT),
        grid=(S // tm,),
        in_specs=[pl.BlockSpec((tm, W_M), lambda i: (i, 0)),
                  pl.BlockSpec((H_M, D_M, n_mem), lambda i: (0, 0, 0)),
                  pl.BlockSpec((H_M, n_mem, D_M), lambda i: (0, 0, 0))],
        out_specs=pl.BlockSpec((tm, W_M), lambda i: (i, 0)),
        compiler_params=_cp("parallel"),
        name="mem_attn",
    )(q_m, k_t, v)


def _float_key(x):
    b = pltpu.bitcast(x, jnp.int32)
    return jnp.where(b >= 0, b, b ^ jnp.int32(0x7FFFFFFF))


def _indexer_body(qi_ref, w_ref, ki_ref, o_ref, keys_ref, cut_ref, *, S, topk, TQ, TK, HC):
    j = pl.program_id(0)
    n_valid = (j + 1) * TQ
    nkt = (n_valid + TK - 1) // TK

    @pl.when(j == 0)
    def _():
        keys_ref[...] = jnp.full(keys_ref.shape, INT_MIN, jnp.int32)

    qpos = j * TQ + lax.broadcasted_iota(jnp.int32, (TK, TQ), 1)
    krow = lax.broadcasted_iota(jnp.int32, (TK, TQ), 0)

    def score_tile(kt, carry):
        r0 = pl.multiple_of(kt * TK, TK)
        k = ki_ref[pl.ds(r0, TK), :]
        acc = jnp.zeros((TK, TQ), F32)
        for hc in range(H_IDX // HC):
            lg = jnp.dot(k, qi_ref[0, :, hc * HC * TQ:(hc + 1) * HC * TQ],
                         preferred_element_type=F32)
            for hh in range(HC):
                h = hc * HC + hh
                acc = acc + jnp.maximum(lg[:, hh * TQ:(hh + 1) * TQ], 0.0) * w_ref[0, h:h + 1, :]
        key = jnp.where(krow + r0 <= qpos, _float_key(acc), INT_MIN)
        keys_ref[pl.ds(r0, TK), :] = key
        return carry

    lax.fori_loop(0, nkt, score_tile, 0)

    def count_ge(cand):
        def body(kt, c):
            r0 = pl.multiple_of(kt * TK, TK)
            ge = jnp.where(keys_ref[pl.ds(r0, TK), :] >= cand, 1.0, 0.0)
            return c + jnp.sum(ge.reshape(TK // 8, 8, TQ), axis=0)
        c8 = lax.fori_loop(0, nkt, body, jnp.zeros((8, TQ), F32))
        return jnp.sum(c8, axis=0, keepdims=True)

    prefix = jnp.zeros((1, TQ), jnp.int32)
    for bit in range(31, -1, -1):
        bv = jnp.int32(INT_MIN) if bit == 31 else jnp.int32(1 << bit)
        cand_u = prefix | bv
        cnt = count_ge(cand_u ^ jnp.int32(INT_MIN))
        prefix = jnp.where(cnt >= float(topk), cand_u, prefix)
    thr = prefix ^ jnp.int32(INT_MIN)

    cnt_gt = count_ge(jnp.where(thr == jnp.int32(0x7FFFFFFF), thr, thr + 1))
    cnt_gt = jnp.where(thr == jnp.int32(0x7FFFFFFF), 0.0, cnt_gt)
    need = float(topk) - cnt_gt
    cnt_ge = count_ge(thr)
    has_tie = jnp.max(jnp.where((cnt_ge > float(topk)) & (thr != jnp.int32(INT_MIN)), 1.0, 0.0)) > 0.0

    def tie_cut():
        def count_eq_le(cut):
            def body(kt, c):
                r0 = pl.multiple_of(kt * TK, TK)
                idx = krow + r0
                hit = jnp.where((keys_ref[pl.ds(r0, TK), :] == thr) & (idx <= cut), 1.0, 0.0)
                return c + jnp.sum(hit.reshape(TK // 8, 8, TQ), axis=0)
            c8 = lax.fori_loop(0, nkt, body, jnp.zeros((8, TQ), F32))
            return jnp.sum(c8, axis=0, keepdims=True)
        lo = jnp.zeros((1, TQ), jnp.int32)
        hi = jnp.full((1, TQ), S - 1, jnp.int32)
        for _ in range(max(1, int(math.ceil(math.log2(S))))):
            mid = (lo + hi) >> 1
            ok = count_eq_le(mid) >= need
            hi = jnp.where(ok, mid, hi)
            lo = jnp.where(ok, lo, mid + 1)
        return hi

    cut_ref[...] = jnp.full((1, TQ), S - 1, jnp.int32)

    @pl.when(has_tie)
    def _():
        cut_ref[...] = tie_cut()

    cut = cut_ref[...]
    thr_eff = jnp.maximum(thr, jnp.int32(INT_MIN + 1))

    def write_tile(kt, carry):
        r0 = pl.multiple_of(kt * TK, TK)
        key = keys_ref[pl.ds(r0, TK), :]
        idx = krow + r0
        sel = (key > thr_eff) | ((key == thr_eff) & (idx <= cut))
        o_ref[pl.ds(r0, TK), :] = jnp.where(sel, 0.0, NEG).astype(o_ref.dtype)
        return carry

    lax.fori_loop(0, S // TK, write_tile, 0)


def _indexer(qi_t, w_t, k_i, S, topk):
    TQ, TK = TQ_IDX, min(TK_IDX, S)
    nqb = S // TQ
    return pl.pallas_call(
        functools.partial(_indexer_body, S=S, topk=topk, TQ=TQ, TK=TK, HC=4),
        out_shape=jax.ShapeDtypeStruct((S, S), jnp.bfloat16),
        grid=(nqb,),
        in_specs=[pl.BlockSpec((1, D_IDX, H_IDX * TQ), lambda j: (j, 0, 0)),
                  pl.BlockSpec((1, H_IDX, TQ), lambda j: (j, 0, 0)),
                  pl.BlockSpec((S, D_IDX), lambda j: (0, 0))],
        out_specs=pl.BlockSpec((S, TQ), lambda j: (0, j)),
        scratch_shapes=[pltpu.VMEM((S, TQ), jnp.int32), pltpu.VMEM((1, TQ), jnp.int32)],
        compiler_params=_cp("arbitrary"),
        name="dsa_indexer",
    )(qi_t, w_t, k_i)


def _flash_step(s, h, vt, m_ref, l_ref, acc_ref):
    m_old = m_ref[h]
    m_new = jnp.maximum(m_old, jnp.max(s, axis=0, keepdims=True))
    a = jnp.exp(m_old - m_new)
    p = jnp.exp(s - m_new)
    l_ref[h] = a * l_ref[h] + jnp.sum(p, axis=0, keepdims=True)
    acc_ref[h] = a * acc_ref[h] + jnp.dot(vt, p.astype(vt.dtype), preferred_element_type=F32)
    m_ref[h] = m_new


def _flash_init(m_ref, l_ref, acc_ref):
    m_ref[...] = jnp.full(m_ref.shape, NEG, F32)
    l_ref[...] = jnp.zeros(l_ref.shape, F32)
    acc_ref[...] = jnp.zeros(acc_ref.shape, F32)


def _dsa_attn_body(q_ref, k_ref, v_ref, b_ref, o_ref, m_ref, l_ref, acc_ref, *, TQ, TK, nkt):
    qb, kt = pl.program_id(0), pl.program_id(1)
    last = ((qb + 1) * TQ - 1) // TK

    @pl.when(kt == 0)
    def _():
        _flash_init(m_ref, l_ref, acc_ref)

    @pl.when(kt <= last)
    def _():
        bias = b_ref[...].astype(F32)
        for h in range(H_A):
            s = jnp.dot(k_ref[h], q_ref[h], preferred_element_type=F32) + bias
            _flash_step(s, h, v_ref[h], m_ref, l_ref, acc_ref)

    @pl.when(kt == nkt - 1)
    def _():
        for h in range(H_A):
            o_ref[h] = (acc_ref[h] / l_ref[h]).astype(o_ref.dtype)


def _dsa_attn(q_t, k_ext, v_t, bias, S):
    TQ, TK = min(TQ_ATT, S), min(TK_ATT, S)
    nqb, nkt = S // TQ, S // TK

    def kmap(qb, kt):
        return jnp.minimum(kt, ((qb + 1) * TQ - 1) // TK)

    return pl.pallas_call(
        functools.partial(_dsa_attn_body, TQ=TQ, TK=TK, nkt=nkt),
        out_shape=jax.ShapeDtypeStruct((H_A, HEAD_DIM, S), CDT),
        grid=(nqb, nkt),
        in_specs=[pl.BlockSpec((H_A, 2 * LANE, TQ), lambda qb, kt: (0, 0, qb)),
                  pl.BlockSpec((H_A, TK, 2 * LANE), lambda qb, kt: (0, kmap(qb, kt), 0)),
                  pl.BlockSpec((H_A, HEAD_DIM, TK), lambda qb, kt: (0, 0, kmap(qb, kt))),
                  pl.BlockSpec((TK, TQ), lambda qb, kt: (kmap(qb, kt), qb))],
        out_specs=pl.BlockSpec((H_A, HEAD_DIM, TQ), lambda qb, kt: (0, 0, qb)),
        scratch_shapes=[pltpu.VMEM((H_A, 1, TQ), F32), pltpu.VMEM((H_A, 1, TQ), F32),
                        pltpu.VMEM((H_A, HEAD_DIM, TQ), F32)],
        compiler_params=_cp("parallel", "arbitrary"),
        name="dsa_attn",
    )(q_t, k_ext, v_t, bias)


def _compress_body(blk_ref, pe_ref, w1_ref, b1_ref, w2_ref, b2_ref, o_ref):
    tb = (blk_ref[0, 0].astype(F32) + pe_ref[0]).astype(w1_ref.dtype)
    hdn = jax.nn.gelu(jnp.dot(tb, w1_ref[0], preferred_element_type=F32) + b1_ref[0])
    out = jnp.dot(hdn.astype(w2_ref.dtype), w2_ref[0], preferred_element_type=F32) + b2_ref[0]
    o_ref[0, 0] = out.astype(o_ref.dtype)


def _compress(blocks, pe, w1, b1, w2, b2):
    _, G, NC, LD = blocks.shape
    hid, D = w2.shape[1], w2.shape[2]
    return pl.pallas_call(
        _compress_body,
        out_shape=jax.ShapeDtypeStruct((2, G, NC, D), CDT),
        grid=(2, G),
        in_specs=[pl.BlockSpec((1, 1, NC, LD), lambda j, g: (j, g, 0, 0)),
                  pl.BlockSpec((1, 1, LD), lambda j, g: (j, 0, 0)),
                  pl.BlockSpec((1, LD, hid), lambda j, g: (j, 0, 0)),
                  pl.BlockSpec((1, 1, hid), lambda j, g: (j, 0, 0)),
                  pl.BlockSpec((1, hid, D), lambda j, g: (j, 0, 0)),
                  pl.BlockSpec((1, 1, D), lambda j, g: (j, 0, 0))],
        out_specs=pl.BlockSpec((1, 1, NC, D), lambda j, g: (j, g, 0, 0)),
        compiler_params=_cp("parallel", "parallel"),
        name="nsa_compress",
    )(blocks, pe, w1, b1, w2, b2)


def _nsa_cmp_body(q_ref, kc_ref, vct_ref, mt_ref, g_ref, oc_ref, sb_ref, *, TQ, NC, NSB, n_sel):
    qb = pl.program_id(0)
    tq = qb * TQ + lax.broadcasted_iota(jnp.int32, (NC, TQ), 1)
    cmp_end = lax.broadcasted_iota(jnp.int32, (NC, TQ), 0) * CMP_STRIDE + (CMP_LEN - 1)
    vis = cmp_end <= tq
    blk = lax.broadcasted_iota(jnp.int32, (NSB, TQ), 0)
    tqs = qb * TQ + lax.broadcasted_iota(jnp.int32, (NSB, TQ), 1)
    cur = jnp.right_shift(tqs, SLC_BLOCK.bit_length() - 1)
    forced = (blk == 0) | (blk == cur) | (blk == cur - 1)
    causal_b = blk * SLC_BLOCK <= tqs
    blk_f = blk.astype(F32)
    for g in range(G_B):
        imp = jnp.zeros((NSB, TQ), F32)
        for hh in range(HPG_B):
            h = g * HPG_B + hh
            s = jnp.dot(kc_ref[g], q_ref[h], preferred_element_type=F32)
            s = jnp.where(vis, s, NEG)
            m = jnp.max(s, axis=0, keepdims=True)
            e = jnp.where(vis, jnp.exp(s - m), 0.0)
            p = e / jnp.maximum(jnp.sum(e, axis=0, keepdims=True), 1e-30)
            pc = p.astype(vct_ref.dtype)
            o = jnp.dot(vct_ref[g], pc, preferred_element_type=F32)
            oc_ref[h] = o * g_ref[h:h + 1, :]
            imp = imp + jnp.dot(mt_ref[...], pc, preferred_element_type=F32)
        imp = jnp.where(forced, jnp.inf, imp)
        imp = jnp.where(causal_b, imp, -jnp.inf)
        bias = jnp.full((NSB, TQ), NEG, F32)
        for _ in range(n_sel):
            m = jnp.max(imp, axis=0, keepdims=True)
            idx = jnp.min(jnp.where(imp == m, blk_f, float(NSB)), axis=0, keepdims=True)
            hit = blk_f == idx
            bias = jnp.where(hit, jnp.where(m > -jnp.inf, 0.0, bias), bias)
            imp = jnp.where(hit, -jnp.inf, imp)
        sb_ref[g] = bias


def _nsa_cmp(q_t, kc, vc_t, m_t, gates_t, S, n_sel):
    TQ = min(TQ_ATT, S)
    NC = kc.shape[1]
    NSB = m_t.shape[0]
    return pl.pallas_call(
        functools.partial(_nsa_cmp_body, TQ=TQ, NC=NC, NSB=NSB, n_sel=n_sel),
        out_shape=[jax.ShapeDtypeStruct((H_B, HEAD_DIM, S), F32),
                   jax.ShapeDtypeStruct((G_B, NSB, S), F32)],
        grid=(S // TQ,),
        in_specs=[pl.BlockSpec((H_B, HEAD_DIM, TQ), lambda i: (0, 0, i)),
                  pl.BlockSpec((G_B, NC, HEAD_DIM), lambda i: (0, 0, 0)),
                  pl.BlockSpec((G_B, HEAD_DIM, NC), lambda i: (0, 0, 0)),
                  pl.BlockSpec((NSB, NC), lambda i: (0, 0)),
                  pl.BlockSpec((N_GATE, TQ), lambda i: (0, i))],
        out_specs=[pl.BlockSpec((H_B, HEAD_DIM, TQ), lambda i: (0, 0, i)),
                   pl.BlockSpec((G_B, NSB, TQ), lambda i: (0, 0, i))],
        compiler_params=_cp("parallel"),
        name="nsa_cmp_select",
    )(q_t, kc, vc_t, m_t, gates_t)


def _nsa_win_body(q_ref, *rest, TQ, nprev):
    k_refs = rest[:nprev + 1]
    v_refs = rest[nprev + 1:2 * nprev + 2]
    g_ref, o_ref = rest[2 * nprev + 2], rest[2 * nprev + 3]
    qb = pl.program_id(0)
    nk = (nprev + 1) * TQ
    kpos = (qb - nprev) * TQ + lax.broadcasted_iota(jnp.int32, (nk, TQ), 0)
    tq = qb * TQ + lax.broadcasted_iota(jnp.int32, (nk, TQ), 1)
    ok = (kpos >= 0) & (kpos <= tq) & (kpos > tq - WINDOW)
    for g in range(G_B):
        ks = [k_refs[b][:, g * HEAD_DIM:(g + 1) * HEAD_DIM] for b in range(nprev + 1)]
        for hh in range(HPG_B):
            h = g * HPG_B + hh
            q = q_ref[h]
            s = jnp.concatenate([jnp.dot(k, q, preferred_element_type=F32) for k in ks], axis=0)
            s = jnp.where(ok, s, NEG)
            m = jnp.max(s, axis=0, keepdims=True)
            e = jnp.where(ok, jnp.exp(s - m), 0.0)
            p = (e / jnp.maximum(jnp.sum(e, axis=0, keepdims=True), 1e-30)).astype(v_refs[0].dtype)
            o = jnp.zeros((HEAD_DIM, TQ), F32)
            for b in range(nprev + 1):
                o = o + jnp.dot(v_refs[b][g], p[b * TQ:(b + 1) * TQ, :], preferred_element_type=F32)
            o_ref[h] = o * g_ref[2 * H_B + h:2 * H_B + h + 1, :]


def _nsa_win(q_t, kvb, vw_t, gates_t, S):
    TQ = min(TQ_ATT, S)
    nprev = WINDOW // TQ
    cw = G_B * HEAD_DIM
    kcol = (2 * 2 * G_B * HEAD_DIM) // cw
    k_specs = [pl.BlockSpec((TQ, cw), lambda i, d=d: (jnp.maximum(i - d, 0), kcol)) for d in range(nprev, -1, -1)]
    v_specs = [pl.BlockSpec((G_B, HEAD_DIM, TQ), lambda i, d=d: (0, 0, jnp.maximum(i - d, 0))) for d in range(nprev, -1, -1)]
    return pl.pallas_call(
        functools.partial(_nsa_win_body, TQ=TQ, nprev=nprev),
        out_shape=jax.ShapeDtypeStruct((H_B, HEAD_DIM, S), F32),
        grid=(S // TQ,),
        in_specs=[pl.BlockSpec((H_B, HEAD_DIM, TQ), lambda i: (0, 0, i))] + k_specs + v_specs
                 + [pl.BlockSpec((N_GATE, TQ), lambda i: (0, i))],
        out_specs=pl.BlockSpec((H_B, HEAD_DIM, TQ), lambda i: (0, 0, i)),
        compiler_params=_cp("parallel"),
        name="nsa_window",
    )(q_t, *([kvb] * (nprev + 1)), *([vw_t] * (nprev + 1)), gates_t)


def _nsa_sel_body(q_ref, k_ref, v_ref, sb_ref, g_ref, oc_ref, ow_ref, o_ref, m_ref, l_ref, acc_ref,
                  *, TQ, TK, nkt):
    qb, kt = pl.program_id(0), pl.program_id(1)
    last = ((qb + 1) * TQ - 1) // TK
    bpt = TK // SLC_BLOCK

    @pl.when(kt == 0)
    def _():
        _flash_init(m_ref, l_ref, acc_ref)

    @pl.when(kt <= last)
    def _():
        kpos = kt * TK + lax.broadcasted_iota(jnp.int32, (TK, TQ), 0)
        tq = qb * TQ + lax.broadcasted_iota(jnp.int32, (TK, TQ), 1)
        causal = kpos <= tq
        b0 = pl.multiple_of(kt * bpt, bpt)
        for g in range(G_B):
            bb = sb_ref[g, pl.ds(b0, bpt), :]
            btok = jnp.concatenate(
                [jnp.broadcast_to(bb[r:r + 1, :], (SLC_BLOCK, TQ)) for r in range(bpt)], axis=0)
            bias = jnp.where(causal, btok, NEG)
            k = k_ref[:, g * HEAD_DIM:(g + 1) * HEAD_DIM]
            for hh in range(HPG_B):
                h = g * HPG_B + hh
                s = jnp.dot(k, q_ref[h], preferred_element_type=F32) + bias
                _flash_step(s, h, v_ref[g], m_ref, l_ref, acc_ref)

    @pl.when(kt == nkt - 1)
    def _():
        for h in range(H_B):
            o = acc_ref[h] / l_ref[h] * g_ref[H_B + h:H_B + h + 1, :] + oc_ref[h] + ow_ref[h]
            o_ref[h] = o.astype(o_ref.dtype)


def _nsa_sel(q_t, kvb, vs_t, sel_bias, gates_t, oc_t, ow_t, S):
    TQ, TK = min(TQ_ATT, S), min(TK_ATT, S)
    nqb, nkt = S // TQ, S // TK
    NSB = sel_bias.shape[1]
    cw = G_B * HEAD_DIM
    kcol = (1 * 2 * G_B * HEAD_DIM) // cw

    def kmap(qb, kt):
        return jnp.minimum(kt, ((qb + 1) * TQ - 1) // TK)

    hblk = pl.BlockSpec((H_B, HEAD_DIM, TQ), lambda qb, kt: (0, 0, qb))
    return pl.pallas_call(
        functools.partial(_nsa_sel_body, TQ=TQ, TK=TK, nkt=nkt),
        out_shape=jax.ShapeDtypeStruct((H_B, HEAD_DIM, S), CDT),
        grid=(nqb, nkt),
        in_specs=[hblk,
                  pl.BlockSpec((TK, cw), lambda qb, kt: (kmap(qb, kt), kcol)),
                  pl.BlockSpec((G_B, HEAD_DIM, TK), lambda qb, kt: (0, 0, kmap(qb, kt))),
                  pl.BlockSpec((G_B, NSB, TQ), lambda qb, kt: (0, 0, qb)),
                  pl.BlockSpec((N_GATE, TQ), lambda qb, kt: (0, qb)),
                  hblk, hblk],
        out_specs=hblk,
        scratch_shapes=[pltpu.VMEM((H_B, 1, TQ), F32), pltpu.VMEM((H_B, 1, TQ), F32),
                        pltpu.VMEM((H_B, HEAD_DIM, TQ), F32)],
        compiler_params=_cp("parallel", "arbitrary"),
        name="nsa_select_attn",
    )(q_t, kvb, vs_t, sel_bias, gates_t, oc_t, ow_t)


def _rot_cols(w, half, pad):
    parts = [-w[..., half:2 * half], w[..., :half]]
    if pad:
        parts.append(jnp.zeros(w.shape[:-1] + (pad,), w.dtype))
    return jnp.concatenate(parts, axis=-1)


def _prep_mixer_weights(w_in, w_uq, w_iq, w_uk, w_uv, gate_b):
    D = w_in.shape[0]
    cuts = np.cumsum(IN_SPLITS)[:-1].tolist()
    c_q, c_kv, k_r, k_idx, w_idx, q_b, kv_b, gate, q_m = jnp.split(w_in, cuts, axis=1)
    z = lambda n: jnp.zeros((D, n), w_in.dtype)
    w_pa = jnp.concatenate([
        c_q, c_kv, k_r, k_idx,
        _rot_cols(k_r, D_ROPE // 2, 0), _rot_cols(k_idx, D_IDX_ROPE // 2, D_IDX - D_IDX_ROPE),
        w_idx, z(LANE - H_IDX), gate, z(LANE - N_GATE)], axis=1).astype(CDT)
    w_pq = jnp.concatenate([q_b, q_m], axis=1).astype(CDT)
    w_pkv = kv_b.astype(CDT)
    gb = jnp.concatenate([gate_b, jnp.zeros((LANE - N_GATE,), F32)]).reshape(1, LANE)

    w3 = w_uq.reshape(Q_LORA, H_A, D_NOPE + D_ROPE)
    nope, pe = w3[..., :D_NOPE], w3[..., D_NOPE:]
    zq = jnp.zeros((Q_LORA, H_A, LANE - D_ROPE), w_uq.dtype)
    main = jnp.concatenate([nope, pe, zq], axis=-1).reshape(Q_LORA, H_A * 2 * LANE)
    rot = _rot_cols(pe, D_ROPE // 2, LANE - D_ROPE).reshape(Q_LORA, H_A * LANE)
    w_q = jnp.concatenate([main, rot], axis=1).astype(CDT)

    wi3 = w_iq.reshape(Q_LORA, H_IDX, D_IDX)
    roti = _rot_cols(wi3, D_IDX_ROPE // 2, D_IDX - D_IDX_ROPE).reshape(Q_LORA, H_IDX * D_IDX)
    w_qi = jnp.concatenate([w_iq, roti], axis=1).astype(CDT)

    w_kv = jnp.concatenate([w_uk.reshape(KV_LORA, W_A), w_uv.reshape(KV_LORA, W_A)], axis=1).astype(CDT)
    return w_pa, w_pq, w_pkv, gb, w_q, w_qi, w_kv


def _cmp_to_slc_t(S):
    n_c = S // CMP_STRIDE
    n_sb = S // SLC_BLOCK
    cs = np.arange(n_c) * CMP_STRIDE
    ss = np.arange(n_sb) * SLC_BLOCK
    m = (cs[None, :] < ss[:, None] + SLC_BLOCK) & (cs[None, :] + CMP_LEN > ss[:, None])
    m[:, (S - CMP_LEN) // CMP_STRIDE + 1:] = False
    return jnp.asarray(m.astype(np.float32)).astype(CDT)


def _mixer(x, xc, mem_c, tabs, lw, S):
    (w_pa, w_pq, w_pkv, gb, w_q, w_qi, w_kv, g_cq, g_ckv, cmp_w1, cmp_b1, cmp_w2, cmp_b2, cmp_pe,
     w_mem_kv, w_out) = lw
    rowtab = lambda t: (tabs[t], "row")

    cq, ckv, kk, wi, gates = _mm_rows(
        xc, w_pa, _epi_dsa_prep,
        [(g_cq.reshape(1, -1), "full"), (g_ckv.reshape(1, -1), "full"), (gb, "full"), rowtab(0), rowtab(1)],
        [(Q_LORA, CDT), (KV_LORA, CDT), (LANE, CDT), (LANE, F32), (LANE, F32)], "proj_dsa")
    q_b, q_m = _mm_rows(xc, w_pq, _epi_q_nsa_mem, [rowtab(6), rowtab(7)],
                        [(W_B, CDT), (W_M, CDT)], "proj_q")
    kvb, = _mm_rows(xc, w_pkv, _epi_kv_nsa, [rowtab(6), rowtab(7)], [(N_KVB, CDT)], "proj_kv")

    q_ext, = _mm_rows(cq, w_q, _epi_dsa_q, [rowtab(2), rowtab(3)], [(H_A * 2 * LANE, CDT)], "dsa_q")
    q_idx, = _mm_rows(cq, w_qi, _epi_idx_q, [rowtab(4), rowtab(5)], [(H_IDX * D_IDX, CDT)], "dsa_qi")
    kv_a = _mm(ckv, w_kv, CDT, "dsa_kv")
    k_pe, k_i = kk[:, :D_ROPE], kk[:, D_ROPE:]
    k_nope = kv_a[:, :W_A].reshape(S, H_A, D_NOPE).transpose(1, 0, 2)
    k_ext = jnp.concatenate([
        k_nope, jnp.broadcast_to(k_pe[None], (H_A, S, D_ROPE)),
        jnp.zeros((H_A, S, LANE - D_ROPE), CDT)], axis=-1)
    va_t = kv_a[:, W_A:].reshape(S, H_A, HEAD_DIM).transpose(1, 2, 0)
    qa_t = q_ext.reshape(S, H_A, 2 * LANE).transpose(1, 2, 0)
    nqi = S // TQ_IDX
    qi_t = q_idx.reshape(nqi, TQ_IDX, H_IDX, D_IDX).transpose(0, 3, 2, 1).reshape(nqi, D_IDX, H_IDX * TQ_IDX)
    w_t = wi[:, :H_IDX].reshape(nqi, TQ_IDX, H_IDX).transpose(0, 2, 1)
    topk = min(TOPK_MAX, S // 4)
    dsa_bias = _indexer(qi_t, w_t, k_i, S, topk)
    oa_t = _dsa_attn(qa_t, k_ext, va_t, dsa_bias, S)
    o_a = oa_t.transpose(2, 0, 1).reshape(S, W_A)

    gates_t = gates[:, :N_GATE].T
    qb_t = q_b.reshape(S, H_B, HEAD_DIM).transpose(1, 2, 0)
    cw = G_B * HEAD_DIM
    nch = S // CMP_STRIDE
    ld = CMP_LEN * HEAD_DIM

    def cmp_blocks(t):
        ch = t.reshape(nch, CMP_STRIDE, G_B, HEAD_DIM).transpose(2, 0, 1, 3).reshape(G_B, nch, CMP_STRIDE * HEAD_DIM)
        nxt = jnp.concatenate([ch[:, 1:], jnp.zeros((G_B, 1, CMP_STRIDE * HEAD_DIM), ch.dtype)], axis=1)
        return jnp.concatenate([ch, nxt], axis=-1)

    blocks = jnp.stack([cmp_blocks(kvb[:, 0:cw]), cmp_blocks(kvb[:, cw:2 * cw])])
    cmp_out = _compress(blocks, cmp_pe.reshape(2, 1, ld), cmp_w1.astype(CDT), cmp_b1.reshape(2, 1, CMP_HID),
                        cmp_w2.astype(CDT), cmp_b2.reshape(2, 1, HEAD_DIM))
    kc = cmp_out[0]
    vc_t = cmp_out[1].transpose(0, 2, 1)
    n_sb = S // SLC_BLOCK
    oc_t, sel_bias = _nsa_cmp(qb_t, kc, vc_t, _cmp_to_slc_t(S), gates_t, S, min(N_SLC, n_sb))
    vs_t = kvb[:, 3 * cw:4 * cw].reshape(S, G_B, HEAD_DIM).transpose(1, 2, 0)
    vw_t = kvb[:, 5 * cw:6 * cw].reshape(S, G_B, HEAD_DIM).transpose(1, 2, 0)
    ow_t = _nsa_win(qb_t, kvb, vw_t, gates_t, S)
    ob_t = _nsa_sel(qb_t, kvb, vs_t, sel_bias, gates_t, oc_t, ow_t, S)
    o_b = ob_t.transpose(2, 0, 1).reshape(S, W_B)

    n_mem = mem_c.shape[0]
    mem_kv = _mm(mem_c, w_mem_kv.astype(CDT), CDT, "mem_kv")
    mk_t = mem_kv[:, :W_M].reshape(n_mem, H_M, D_M).transpose(1, 2, 0)
    mv = mem_kv[:, W_M:].reshape(n_mem, H_M, D_M).transpose(1, 0, 2)
    o_m = _mem_attn(q_m, mk_t, mv)

    mix = _mm(jnp.concatenate([o_a, o_b, o_m], axis=1), w_out.astype(CDT), F32, "mix_out")
    return mix


def kernel(x, mem, positions, w_in, g_cq, g_ckv, w_uq, w_iq, w_uk, w_uv, cmp_w1, cmp_b1, cmp_w2, cmp_b2,
           cmp_pe, gate_b, w_mem_kv, w_out, ln1_g, ln1_b, ln2_g, ln2_b, ffn_wg, ffn_wu, ffn_wd,
           router_w, moe_wg, moe_wu, moe_wd):
    B, S, D = x.shape
    assert B == 1 and S % max(TK_ATT, TK_IDX) == 0 and WINDOW % TQ_ATT == 0
    depth = w_in.shape[0]
    alpha = (2 * depth) ** 0.25
    xs = x[0]
    mem_c = mem[0].astype(CDT)
    tabs = _rope_tables(positions)
    for i in range(depth):
        lw = _prep_mixer_weights(w_in[i], w_uq[i], w_iq[i], w_uk[i], w_uv[i], gate_b[i]) + (
            g_cq[i], g_ckv[i], cmp_w1[i], cmp_b1[i], cmp_w2[i], cmp_b2[i], cmp_pe[i], w_mem_kv[i], w_out[i])
        mix = _mixer(xs, xs.astype(CDT), mem_c, tabs, lw, S)
        xs = _res_ln(xs, mix, ln1_g[i], ln1_b[i], alpha)
        xc = xs.astype(CDT)
        j = i // 2
        if i % 2 == 0:
            h = _swiglu_up(xc, ffn_wg[j].astype(CDT), ffn_wu[j].astype(CDT), "ffn_up")
            f = _mm(h, ffn_wd[j].astype(CDT), F32, "ffn_down")
        else:
            n_exp, _, dfe = moe_wg[j].shape[0], moe_wg[j].shape[1], moe_wg[j].shape[2]
            rw = jnp.concatenate([router_w[j], jnp.zeros((D, LANE - n_exp), F32)], axis=1).astype(CDT)
            gate, = _mm_rows(xc, rw, _make_epi_router(n_exp), [], [(LANE, F32)], "moe_router")
            scale = gate[:, :n_exp].T.reshape(n_exp, S, 1)
            wg = moe_wg[j].transpose(1, 0, 2).reshape(D, n_exp * dfe).astype(CDT)
            wu = moe_wu[j].transpose(1, 0, 2).reshape(D, n_exp * dfe).astype(CDT)
            h = _swiglu_up(xc, wg, wu, "moe_up", scale=scale, n_per_scale=dfe)
            f = _mm(h, moe_wd[j].reshape(n_exp * dfe, D).astype(CDT), F32, "moe_down")
        xs = _res_ln(xs, f, ln2_g[i], ln2_b[i], alpha)
    return xs[None]
```

```python
import functools
import math

import numpy as np
import jax
import jax.numpy as jnp
from jax import lax
from jax.experimental import pallas as pl
from jax.experimental.pallas import tpu as pltpu

F32 = jnp.float32
CDT = jnp.bfloat16

HEAD_DIM = 128
H_A = 12
H_B = 12
H_M = 4
D_M = 256
Q_LORA = 1024
KV_LORA = 512
D_NOPE = 128
D_ROPE = 64
H_IDX = 32
D_IDX = 64
D_IDX_ROPE = 32
TOPK_MAX = 256
G_B = 4
HPG_B = H_B // G_B
CMP_LEN = 32
CMP_STRIDE = 16
CMP_HID = 256
SLC_BLOCK = 64
N_SLC = 16
WINDOW = 512
TOP_E = 2
ROPE_THETA = 10000.0
EPS = 1e-5

W_A = H_A * HEAD_DIM
W_B = H_B * HEAD_DIM
W_M = H_M * D_M
N_KVB = 3 * 2 * G_B * HEAD_DIM
N_GATE = 3 * H_B
IN_SPLITS = (Q_LORA, KV_LORA, D_ROPE, D_IDX, H_IDX, W_B, N_KVB, N_GATE, W_M)

LANE = 128
NEG = -1e30
LOG2E = math.log2(math.e)
INT_MIN = -2147483648
VMEM_LIMIT = 48 * 1024 * 1024

TQ_IDX = 128
TK_IDX = 512
TQ_ATT = 256
TK_ATT = 1024


def _cp(*sem):
    return pltpu.CompilerParams(dimension_semantics=sem, vmem_limit_bytes=VMEM_LIMIT)


def _pick(n, cands):
    for c in cands:
        if c <= n and n % c == 0:
            return c
    return n


def _mm_body(x_ref, w_ref, o_ref, *scratch, nk):
    part = jnp.dot(x_ref[...], w_ref[...], preferred_element_type=F32)
    if nk == 1:
        o_ref[...] = part.astype(o_ref.dtype)
        return
    acc_ref, = scratch
    k = pl.program_id(2)

    @pl.when(k == 0)
    def _():
        acc_ref[...] = part

    @pl.when(k > 0)
    def _():
        acc_ref[...] += part

    @pl.when(k == nk - 1)
    def _():
        o_ref[...] = acc_ref[...].astype(o_ref.dtype)


def _mm(x, w, out_dtype, name):
    M, K = x.shape
    N = w.shape[1]
    tm = _pick(M, (512, 256, 128))
    tn = _pick(N, (1024, 512, 256, 128))
    tk = K
    for parts in (1, 2, 4, 8, 16):
        if K % parts == 0 and (K // parts) % LANE == 0 and (K // parts) * max(tm, tn) * 2 <= 12 * 2 ** 20:
            tk = K // parts
            break
    nk = K // tk
    scratch = [pltpu.VMEM((tm, tn), F32)] if nk > 1 else []
    return pl.pallas_call(
        functools.partial(_mm_body, nk=nk),
        out_shape=jax.ShapeDtypeStruct((M, N), out_dtype),
        grid=(M // tm, N // tn, nk),
        in_specs=[pl.BlockSpec((tm, tk), lambda i, j, k: (i, k)),
                  pl.BlockSpec((tk, tn), lambda i, j, k: (k, j))],
        out_specs=pl.BlockSpec((tm, tn), lambda i, j, k: (i, j)),
        scratch_shapes=scratch,
        compiler_params=_cp("parallel", "parallel", "arbitrary"),
        name=name,
    )(x, w)


def _mm_rows_body(*refs, nk, n_aux, n_out, epi):
    x_ref, w_ref = refs[0], refs[1]
    aux = refs[2:2 + n_aux]
    outs = refs[2 + n_aux:2 + n_aux + n_out]
    acc_ref = refs[2 + n_aux + n_out]
    k = pl.program_id(1)
    N = w_ref.shape[1]
    cw = _pick(N, (512, 256, 128))
    x = x_ref[...]

    @pl.when(k == 0)
    def _():
        for c in range(0, N, cw):
            acc_ref[:, c:c + cw] = jnp.dot(x, w_ref[:, c:c + cw], preferred_element_type=F32)

    @pl.when(k > 0)
    def _():
        for c in range(0, N, cw):
            acc_ref[:, c:c + cw] += jnp.dot(x, w_ref[:, c:c + cw], preferred_element_type=F32)

    @pl.when(k == nk - 1)
    def _():
        epi(acc_ref, aux, outs)


def _mm_rows(x, w, epi, aux, outs, name, tm=None):
    M, K = x.shape
    N = w.shape[1]
    tm = tm or _pick(M, (512, 256, 128))
    tk = _pick(K, (512, 256, 128))
    nk = K // tk
    in_specs = [pl.BlockSpec((tm, tk), lambda i, k: (i, k)),
                pl.BlockSpec((tk, N), lambda i, k: (k, 0))]
    for a, kind in aux:
        if kind == "row":
            in_specs.append(pl.BlockSpec((tm, a.shape[1]), lambda i, k: (i, 0)))
        else:
            in_specs.append(pl.BlockSpec(a.shape, lambda i, k, nd=a.ndim: (0,) * nd))
    res = pl.pallas_call(
        functools.partial(_mm_rows_body, nk=nk, n_aux=len(aux), n_out=len(outs), epi=epi),
        out_shape=[jax.ShapeDtypeStruct((M, c), dt) for c, dt in outs],
        grid=(M // tm, nk),
        in_specs=in_specs,
        out_specs=[pl.BlockSpec((tm, c), lambda i, k: (i, 0)) for c, _ in outs],
        scratch_shapes=[pltpu.VMEM((tm, N), F32)],
        compiler_params=_cp("parallel", "arbitrary"),
        name=name,
    )(x, w, *[a for a, _ in aux])
    return res


def _epi_plain(acc_ref, aux, outs):
    outs[0][...] = acc_ref[...].astype(outs[0].dtype)


def _rope_tab_body(pos_ref, inv_ref, sgn_ref, o_ref):
    pos = pos_ref[...].astype(F32)
    nt = inv_ref.shape[0]
    for t in range(nt):
        ang = pos * inv_ref[t:t + 1, :]
        o_ref[2 * t] = jnp.cos(ang)
        o_ref[2 * t + 1] = jnp.sin(ang) * sgn_ref[t:t + 1, :]


def _inv_freq(d):
    return jnp.power(ROPE_THETA, -jnp.arange(0, d, 2, dtype=F32) / d)


def _rope_tables(positions):
    S = positions.shape[1]
    i32, i16, i64 = _inv_freq(D_ROPE), _inv_freq(D_IDX_ROPE), _inv_freq(HEAD_DIM)
    z32, z64 = jnp.zeros((32,), F32), jnp.zeros((64,), F32)
    inv = jnp.stack([
        jnp.concatenate([i32, i32, i16, i16, z32]),
        jnp.concatenate([i32, i32, z64]),
        jnp.concatenate([i16, i16, z32, i16, i16, z32]),
        jnp.concatenate([i64, i64]),
    ])
    sgn = jnp.ones((4, LANE), F32).at[3, :64].set(-1.0)
    tm = _pick(S, (512, 256, 128))
    return pl.pallas_call(
        _rope_tab_body,
        out_shape=jax.ShapeDtypeStruct((8, S, LANE), F32),
        grid=(S // tm,),
        in_specs=[pl.BlockSpec((tm, 1), lambda i: (i, 0)),
                  pl.BlockSpec((4, LANE), lambda i: (0, 0)),
                  pl.BlockSpec((4, LANE), lambda i: (0, 0))],
        out_specs=pl.BlockSpec((8, tm, LANE), lambda i: (0, i, 0)),
        compiler_params=_cp("parallel"),
        name="rope_tables",
    )(positions.reshape(S, 1), inv, sgn)


N_PA = Q_LORA + KV_LORA + 4 * LANE


def _rms(v, g):
    return v * lax.rsqrt(jnp.mean(v * v, axis=-1, keepdims=True) + EPS) * g


def _epi_dsa_prep(acc_ref, aux, outs):
    gq_ref, gkv_ref, gb_ref, tc_ref, ts_ref = aux
    cq_ref, ckv_ref, kk_ref, wi_ref, gate_ref = outs
    o = Q_LORA + KV_LORA
    cq_ref[...] = _rms(acc_ref[:, 0:Q_LORA], gq_ref[...]).astype(cq_ref.dtype)
    ckv_ref[...] = _rms(acc_ref[:, Q_LORA:o], gkv_ref[...]).astype(ckv_ref.dtype)
    kk = acc_ref[:, o:o + LANE] * tc_ref[...] + acc_ref[:, o + LANE:o + 2 * LANE] * ts_ref[...]
    kk_ref[...] = kk.astype(kk_ref.dtype)
    wi_ref[...] = acc_ref[:, o + 2 * LANE:o + 3 * LANE] * (H_IDX ** -0.5 * D_IDX ** -0.5)
    gate_ref[...] = jax.nn.sigmoid(acc_ref[:, o + 3 * LANE:o + 4 * LANE] + gb_ref[...])


def _epi_q_nsa_mem(acc_ref, aux, outs):
    tc_ref, ts_ref = aux
    qb_ref, qm_ref = outs
    tc, ts = tc_ref[...], ts_ref[...]
    for h in range(H_B):
        t = acc_ref[:, h * HEAD_DIM:(h + 1) * HEAD_DIM]
        r = (t * tc + pltpu.roll(t, HEAD_DIM // 2, 1) * ts) * (HEAD_DIM ** -0.5 * LOG2E)
        qb_ref[:, h * HEAD_DIM:(h + 1) * HEAD_DIM] = r.astype(qb_ref.dtype)
    qm_ref[...] = (acc_ref[:, W_B:W_B + W_M] * (D_M ** -0.5 * LOG2E)).astype(qm_ref.dtype)


def _epi_kv_nsa(acc_ref, aux, outs):
    tc_ref, ts_ref = aux
    o_ref, = outs
    tc, ts = tc_ref[...], ts_ref[...]
    for br in range(3):
        for kv in range(2):
            for g in range(G_B):
                c = ((br * 2 + kv) * G_B + g) * HEAD_DIM
                t = acc_ref[:, c:c + HEAD_DIM]
                if kv == 0:
                    t = t * tc + pltpu.roll(t, HEAD_DIM // 2, 1) * ts
                o_ref[:, c:c + HEAD_DIM] = t.astype(o_ref.dtype)


def _epi_dsa_q(acc_ref, aux, outs):
    tc_ref, ts_ref = aux
    o_ref, = outs
    tc, ts = tc_ref[...], ts_ref[...]
    sc = (D_NOPE + D_ROPE) ** -0.5 * LOG2E
    W = 2 * LANE
    for h in range(H_A):
        o_ref[:, h * W:h * W + LANE] = (acc_ref[:, h * W:h * W + LANE] * sc).astype(o_ref.dtype)
        pe = acc_ref[:, h * W + LANE:(h + 1) * W] * tc + acc_ref[:, H_A * W + h * LANE:H_A * W + (h + 1) * LANE] * ts
        o_ref[:, h * W + LANE:(h + 1) * W] = (pe * sc).astype(o_ref.dtype)


def _epi_idx_q(acc_ref, aux, outs):
    tc_ref, ts_ref = aux
    o_ref, = outs
    tc, ts = tc_ref[...], ts_ref[...]
    n = H_IDX * D_IDX
    for c in range(n // LANE):
        r = acc_ref[:, c * LANE:(c + 1) * LANE] * tc + acc_ref[:, n + c * LANE:n + (c + 1) * LANE] * ts
        o_ref[:, c * LANE:(c + 1) * LANE] = r.astype(o_ref.dtype)


def _res_ln_body(x_ref, f_ref, g_ref, b_ref, o_ref, *, alpha):
    v = alpha * x_ref[...] + f_ref[...].astype(F32)
    mu = jnp.mean(v, axis=-1, keepdims=True)
    d = v - mu
    var = jnp.mean(d * d, axis=-1, keepdims=True)
    o_ref[...] = d * lax.rsqrt(var + EPS) * g_ref[...] + b_ref[...]


def _res_ln(x, f, g, b, alpha):
    S, D = x.shape
    tm = _pick(S, (256, 128))
    row = pl.BlockSpec((tm, D), lambda i: (i, 0))
    vec = pl.BlockSpec((1, D), lambda i: (0, 0))
    return pl.pallas_call(
        functools.partial(_res_ln_body, alpha=alpha),
        out_shape=jax.ShapeDtypeStruct((S, D), F32),
        grid=(S // tm,),
        in_specs=[row, row, vec, vec],
        out_specs=row,
        compiler_params=_cp("parallel"),
        name="res_ln",
    )(x, f, g.reshape(1, D), b.reshape(1, D))


def _swiglu_up_body(x_ref, wg_ref, wu_ref, *rest, scaled):
    if scaled:
        s_ref, o_ref = rest
    else:
        o_ref, = rest
    x = x_ref[...]
    g = jnp.dot(x, wg_ref[...], preferred_element_type=F32)
    u = jnp.dot(x, wu_ref[...], preferred_element_type=F32)
    h = g * jax.nn.sigmoid(g) * u
    if scaled:
        h = h * s_ref[...]
    o_ref[...] = h.astype(o_ref.dtype)


def _swiglu_up(x, wg, wu, name, scale=None, n_per_scale=None):
    M, K = x.shape
    N = wg.shape[1]
    tm = _pick(M, (512, 256, 128))
    if scale is None:
        tn = _pick(N, (512, 256, 128))
    else:
        tn = _pick(n_per_scale, (512, 256, 128))
    in_specs = [pl.BlockSpec((tm, K), lambda i, j: (i, 0)),
                pl.BlockSpec((K, tn), lambda i, j: (0, j)),
                pl.BlockSpec((K, tn), lambda i, j: (0, j))]
    args = [x, wg, wu]
    if scale is not None:
        per = n_per_scale // tn
        in_specs.append(pl.BlockSpec((None, tm, 1), lambda i, j: (j // per, i, 0)))
        args.append(scale)
    return pl.pallas_call(
        functools.partial(_swiglu_up_body, scaled=scale is not None),
        out_shape=jax.ShapeDtypeStruct((M, N), CDT),
        grid=(M // tm, N // tn),
        in_specs=in_specs,
        out_specs=pl.BlockSpec((tm, tn), lambda i, j: (i, j)),
        compiler_params=_cp("parallel", "parallel"),
        name=name,
    )(*args)


def _make_epi_router(n_exp):
    def epi(acc_ref, aux, outs):
        o_ref, = outs
        lg = acc_ref[...]
        lane = lax.broadcasted_iota(jnp.int32, lg.shape, 1)
        lg = jnp.where(lane < n_exp, lg, -jnp.inf)
        m1 = jnp.max(lg, axis=-1, keepdims=True)
        i1 = jnp.min(jnp.where(lg == m1, lane, LANE), axis=-1, keepdims=True)
        l2 = jnp.where(lane == i1, -jnp.inf, lg)
        m2 = jnp.max(l2, axis=-1, keepdims=True)
        i2 = jnp.min(jnp.where(l2 == m2, lane, LANE), axis=-1, keepdims=True)
        e2 = jnp.exp(m2 - m1)
        den = 1.0 + e2
        o_ref[...] = jnp.where(lane == i1, 1.0 / den, 0.0) + jnp.where(lane == i2, e2 / den, 0.0)
    return epi


def _mem_attn_body(q_ref, kt_ref, v_ref, o_ref):
    for h in range(H_M):
        q = q_ref[:, h * D_M:(h + 1) * D_M]
        s = jnp.dot(q, kt_ref[h], preferred_element_type=F32)
        m = jnp.max(s, axis=-1, keepdims=True)
        e = jnp.exp2(s - m)
        p = e / jnp.sum(e, axis=-1, keepdims=True)
        o = jnp.dot(p.astype(v_ref.dtype), v_ref[h], preferred_element_type=F32)
        o_ref[:, h * D_M:(h + 1) * D_M] = o.astype(o_ref.dtype)


def _mem_attn(q_m, k_t, v):
    S = q_m.shape[0]
    n_mem = v.shape[1]
    tm = _pick(S, (512, 256, 128))
    return pl.pallas_call(
        _mem_attn_body,
        out_shape=jax.ShapeDtypeStruct((S, W_M), CDT),
        grid=(S // tm,),
        in_specs=[pl.BlockSpec((tm, W_M), lambda i: (i, 0)),
                  pl.BlockSpec((H_M, D_M, n_mem), lambda i: (0, 0, 0)),
                  pl.BlockSpec((H_M, n_mem, D_M), lambda i: (0, 0, 0))],
        out_specs=pl.BlockSpec((tm, W_M), lambda i: (i, 0)),
        compiler_params=_cp("parallel"),
        name="mem_attn",
    )(q_m, k_t, v)


def _float_key(x):
    b = pltpu.bitcast(x, jnp.int32)
    return jnp.where(b >= 0, b, b ^ jnp.int32(0x7FFFFFFF))


def _indexer_body(qi_ref, w_ref, ki_ref, o_ref, keys_ref, *, S, topk, TQ, TK, HC):
    j = pl.program_id(0)
    n_valid = (j + 1) * TQ
    nkt = (n_valid + TK - 1) // TK

    @pl.when(j == 0)
    def _():
        keys_ref[...] = jnp.full(keys_ref.shape, INT_MIN, jnp.int32)

    qpos = j * TQ + lax.broadcasted_iota(jnp.int32, (TK, TQ), 1)
    krow = lax.broadcasted_iota(jnp.int32, (TK, TQ), 0)

    def score_tile(kt, carry):
        r0 = pl.multiple_of(kt * TK, TK)
        k = ki_ref[pl.ds(r0, TK), :]
        acc = jnp.zeros((TK, TQ), F32)
        for hc in range(H_IDX // HC):
            lg = jnp.dot(k, qi_ref[0, :, hc * HC * TQ:(hc + 1) * HC * TQ],
                         preferred_element_type=F32)
            for hh in range(HC):
                h = hc * HC + hh
                acc = acc + jnp.maximum(lg[:, hh * TQ:(hh + 1) * TQ], 0.0) * w_ref[0, h:h + 1, :]
        key = jnp.where(krow + r0 <= qpos, _float_key(acc), INT_MIN)
        keys_ref[pl.ds(r0, TK), :] = key
        return carry

    lax.fori_loop(0, nkt, score_tile, 0)

    RB = 64

    def count_where(pred):
        def body(kt, c):
            r0 = pl.multiple_of(kt * TK, TK)
            hit = jnp.where(pred(keys_ref[pl.ds(r0, TK), :], r0), 1.0, 0.0)
            return c + jnp.sum(hit.reshape(TK // RB, RB, TQ), axis=0)
        c = lax.fori_loop(0, nkt, body, jnp.zeros((RB, TQ), F32))
        return jnp.sum(c, axis=0, keepdims=True)

    def count_ge(cand):
        return count_where(lambda key, r0: key >= cand)

    BPG = 4

    def bit_group(carry):
        g, prefix, cnt_p = carry
        for i in range(BPG):
            bv = jnp.left_shift(jnp.int32(1), 31 - (g * BPG + i))
            cand_u = prefix | bv
            cnt = count_ge(cand_u ^ jnp.int32(INT_MIN))
            ok = cnt >= float(topk)
            prefix = jnp.where(ok, cand_u, prefix)
            cnt_p = jnp.where(ok, cnt, cnt_p)
        return g + 1, prefix, cnt_p

    def undecided(carry):
        g, _, cnt_p = carry
        return (g < 32 // BPG) & (jnp.max(jnp.where(cnt_p == float(topk), 0.0, 1.0)) > 0.0)

    start = (jnp.int32(0), jnp.zeros((1, TQ), jnp.int32), jnp.full((1, TQ), nkt * TK, jnp.int32).astype(F32))
    _, prefix, cnt_ge = lax.while_loop(undecided, bit_group, start)
    thr = prefix ^ jnp.int32(INT_MIN)
    thr_eff = jnp.maximum(thr, jnp.int32(INT_MIN + 1))
    has_tie = jnp.max(jnp.where((cnt_ge > float(topk)) & (thr != jnp.int32(INT_MIN)), 1.0, 0.0)) > 0.0

    def write_rows(select):
        def write_tile(kt, carry):
            r0 = pl.multiple_of(kt * TK, TK)
            sel = select(keys_ref[pl.ds(r0, TK), :], r0)
            o_ref[pl.ds(r0, TK), :] = jnp.where(sel, 0.0, NEG).astype(o_ref.dtype)
            return carry
        lax.fori_loop(0, nkt, write_tile, 0)

    @pl.when(jnp.logical_not(has_tie))
    def _():
        write_rows(lambda key, r0: key >= thr_eff)

    @pl.when(has_tie)
    def _():
        top = thr == jnp.int32(0x7FFFFFFF)
        cnt_gt = jnp.where(top, 0.0, count_ge(jnp.where(top, thr, thr + 1)))
        need = float(topk) - cnt_gt
        lo = jnp.zeros((1, TQ), jnp.int32)
        hi = jnp.full((1, TQ), S - 1, jnp.int32)
        for _ in range(max(1, int(math.ceil(math.log2(S))))):
            mid = (lo + hi) >> 1
            ok = count_where(lambda key, r0: (key == thr) & (krow + r0 <= mid)) >= need
            hi = jnp.where(ok, mid, hi)
            lo = jnp.where(ok, lo, mid + 1)
        write_rows(lambda key, r0: (key > thr_eff) | ((key == thr_eff) & (krow + r0 <= hi)))

    def write_masked(kt, carry):
        o_ref[pl.ds(pl.multiple_of(kt * TK, TK), TK), :] = jnp.full((TK, TQ), NEG, o_ref.dtype)
        return carry

    lax.fori_loop(nkt, S // TK, write_masked, 0)


def _indexer(qi_t, w_t, k_i, S, topk):
    TQ, TK = TQ_IDX, min(TK_IDX, S)
    nqb = S // TQ
    return pl.pallas_call(
        functools.partial(_indexer_body, S=S, topk=topk, TQ=TQ, TK=TK, HC=4),
        out_shape=jax.ShapeDtypeStruct((S, S), jnp.bfloat16),
        grid=(nqb,),
        in_specs=[pl.BlockSpec((1, D_IDX, H_IDX * TQ), lambda j: (j, 0, 0)),
                  pl.BlockSpec((1, H_IDX, TQ), lambda j: (j, 0, 0)),
                  pl.BlockSpec((S, D_IDX), lambda j: (0, 0))],
        out_specs=pl.BlockSpec((S, TQ), lambda j: (0, j)),
        scratch_shapes=[pltpu.VMEM((S, TQ), jnp.int32)],
        compiler_params=_cp("arbitrary"),
        name="dsa_indexer",
    )(qi_t, w_t, k_i)


def _flash_heads(n_heads, score_fn, vt_fn, m_ref, l_ref, acc_ref, s_ref, p_ref):
    s_ref[0] = score_fn(0)
    for h in range(n_heads):
        if h + 1 < n_heads:
            s_ref[(h + 1) % 2] = score_fn(h + 1)
        s = s_ref[h % 2]
        m_old = m_ref[h]
        m_new = jnp.maximum(m_old, jnp.max(s, axis=0, keepdims=True))
        a = jnp.exp2(m_old - m_new)
        p = jnp.exp2(s - m_new)
        l_ref[h] = a * l_ref[h] + jnp.sum(p, axis=0, keepdims=True)
        p_ref[h % 2] = p.astype(p_ref.dtype)
        acc_ref[h] = a * acc_ref[h] + jnp.dot(vt_fn(h), p_ref[h % 2], preferred_element_type=F32)
        m_ref[h] = m_new


def _flash_init(m_ref, l_ref, acc_ref):
    m_ref[...] = jnp.full(m_ref.shape, NEG, F32)
    l_ref[...] = jnp.zeros(l_ref.shape, F32)
    acc_ref[...] = jnp.zeros(acc_ref.shape, F32)


def _flash_scratch(n_heads, dv, TQ, TK):
    return [pltpu.VMEM((n_heads, 1, TQ), F32), pltpu.VMEM((n_heads, 1, TQ), F32),
            pltpu.VMEM((n_heads, dv, TQ), F32), pltpu.VMEM((2, TK, TQ), F32), pltpu.VMEM((2, TK, TQ), CDT)]


def _dsa_attn_body(q_ref, k_ref, v_ref, b_ref, o_ref, m_ref, l_ref, acc_ref, s_ref, p_ref, bias_ref,
                   *, TQ, TK, nkt):
    qb, kt = pl.program_id(0), pl.program_id(1)
    last = ((qb + 1) * TQ - 1) // TK

    @pl.when(kt == 0)
    def _():
        _flash_init(m_ref, l_ref, acc_ref)

    @pl.when(kt <= last)
    def _():
        bias_ref[...] = b_ref[...].astype(F32)
        _flash_heads(
            H_A,
            lambda h: jnp.dot(k_ref[h], q_ref[h], preferred_element_type=F32) + bias_ref[...],
            lambda h: v_ref[h], m_ref, l_ref, acc_ref, s_ref, p_ref)

    @pl.when(kt == nkt - 1)
    def _():
        for h in range(H_A):
            o_ref[h] = (acc_ref[h] / l_ref[h]).astype(o_ref.dtype)


def _dsa_attn(q_t, k_ext, v_t, bias, S):
    TQ, TK = min(TQ_ATT, S), min(TK_ATT, S)
    nqb, nkt = S // TQ, S // TK

    def kmap(qb, kt):
        return jnp.minimum(kt, ((qb + 1) * TQ - 1) // TK)

    return pl.pallas_call(
        functools.partial(_dsa_attn_body, TQ=TQ, TK=TK, nkt=nkt),
        out_shape=jax.ShapeDtypeStruct((H_A, HEAD_DIM, S), CDT),
        grid=(nqb, nkt),
        in_specs=[pl.BlockSpec((H_A, 2 * LANE, TQ), lambda qb, kt: (0, 0, qb)),
                  pl.BlockSpec((H_A, TK, 2 * LANE), lambda qb, kt: (0, kmap(qb, kt), 0)),
                  pl.BlockSpec((H_A, HEAD_DIM, TK), lambda qb, kt: (0, 0, kmap(qb, kt))),
                  pl.BlockSpec((TK, TQ), lambda qb, kt: (kmap(qb, kt), qb))],
        out_specs=pl.BlockSpec((H_A, HEAD_DIM, TQ), lambda qb, kt: (0, 0, qb)),
        scratch_shapes=_flash_scratch(H_A, HEAD_DIM, TQ, TK) + [pltpu.VMEM((TK, TQ), F32)],
        compiler_params=_cp("parallel", "arbitrary"),
        name="dsa_attn",
    )(q_t, k_ext, v_t, bias)


def _compress_body(blk_ref, pe_ref, w1_ref, b1_ref, w2_ref, b2_ref, o_ref):
    tb = (blk_ref[0, 0].astype(F32) + pe_ref[0]).astype(w1_ref.dtype)
    hdn = jax.nn.gelu(jnp.dot(tb, w1_ref[0], preferred_element_type=F32) + b1_ref[0])
    out = jnp.dot(hdn.astype(w2_ref.dtype), w2_ref[0], preferred_element_type=F32) + b2_ref[0]
    o_ref[0, 0] = out.astype(o_ref.dtype)


def _compress(blocks, pe, w1, b1, w2, b2):
    _, G, NC, LD = blocks.shape
    hid, D = w2.shape[1], w2.shape[2]
    return pl.pallas_call(
        _compress_body,
        out_shape=jax.ShapeDtypeStruct((2, G, NC, D), CDT),
        grid=(2, G),
        in_specs=[pl.BlockSpec((1, 1, NC, LD), lambda j, g: (j, g, 0, 0)),
                  pl.BlockSpec((1, 1, LD), lambda j, g: (j, 0, 0)),
                  pl.BlockSpec((1, LD, hid), lambda j, g: (j, 0, 0)),
                  pl.BlockSpec((1, 1, hid), lambda j, g: (j, 0, 0)),
                  pl.BlockSpec((1, hid, D), lambda j, g: (j, 0, 0)),
                  pl.BlockSpec((1, 1, D), lambda j, g: (j, 0, 0))],
        out_specs=pl.BlockSpec((1, 1, NC, D), lambda j, g: (j, g, 0, 0)),
        compiler_params=_cp("parallel", "parallel"),
        name="nsa_compress",
    )(blocks, pe, w1, b1, w2, b2)


def _nsa_cmp_body(q_ref, kc_ref, vct_ref, mt_ref, g_ref, oc_ref, sb_ref, *, TQ, NC, NSB, n_sel):
    qb = pl.program_id(0)
    tq = qb * TQ + lax.broadcasted_iota(jnp.int32, (NC, TQ), 1)
    cmp_end = lax.broadcasted_iota(jnp.int32, (NC, TQ), 0) * CMP_STRIDE + (CMP_LEN - 1)
    vis = cmp_end <= tq
    blk = lax.broadcasted_iota(jnp.int32, (NSB, TQ), 0)
    tqs = qb * TQ + lax.broadcasted_iota(jnp.int32, (NSB, TQ), 1)
    cur = jnp.right_shift(tqs, SLC_BLOCK.bit_length() - 1)
    forced = (blk == 0) | (blk == cur) | (blk == cur - 1)
    causal_b = blk * SLC_BLOCK <= tqs
    blk_f = blk.astype(F32)
    for g in range(G_B):
        imp = jnp.zeros((NSB, TQ), F32)
        for hh in range(HPG_B):
            h = g * HPG_B + hh
            s = jnp.dot(kc_ref[g], q_ref[h], preferred_element_type=F32)
            s = jnp.where(vis, s, NEG)
            m = jnp.max(s, axis=0, keepdims=True)
            e = jnp.where(vis, jnp.exp2(s - m), 0.0)
            p = e / jnp.maximum(jnp.sum(e, axis=0, keepdims=True), 1e-30)
            pc = p.astype(vct_ref.dtype)
            o = jnp.dot(vct_ref[g], pc, preferred_element_type=F32)
            oc_ref[h] = o * g_ref[h:h + 1, :]
            imp = imp + jnp.dot(mt_ref[...], pc, preferred_element_type=F32)
        imp = jnp.where(forced, jnp.inf, imp)
        imp = jnp.where(causal_b, imp, -jnp.inf)
        bias = jnp.full((NSB, TQ), NEG, F32)
        for _ in range(n_sel):
            m = jnp.max(imp, axis=0, keepdims=True)
            idx = jnp.min(jnp.where(imp == m, blk_f, float(NSB)), axis=0, keepdims=True)
            hit = blk_f == idx
            bias = jnp.where(hit, jnp.where(m > -jnp.inf, 0.0, bias), bias)
            imp = jnp.where(hit, -jnp.inf, imp)
        sb_ref[g] = bias


def _nsa_cmp(q_t, kc, vc_t, m_t, gates_t, S, n_sel):
    TQ = min(TQ_ATT, S)
    NC = kc.shape[1]
    NSB = m_t.shape[0]
    return pl.pallas_call(
        functools.partial(_nsa_cmp_body, TQ=TQ, NC=NC, NSB=NSB, n_sel=n_sel),
        out_shape=[jax.ShapeDtypeStruct((H_B, HEAD_DIM, S), F32),
                   jax.ShapeDtypeStruct((G_B, NSB, S), F32)],
        grid=(S // TQ,),
        in_specs=[pl.BlockSpec((H_B, HEAD_DIM, TQ), lambda i: (0, 0, i)),
                  pl.BlockSpec((G_B, NC, HEAD_DIM), lambda i: (0, 0, 0)),
                  pl.BlockSpec((G_B, HEAD_DIM, NC), lambda i: (0, 0, 0)),
                  pl.BlockSpec((NSB, NC), lambda i: (0, 0)),
                  pl.BlockSpec((N_GATE, TQ), lambda i: (0, i))],
        out_specs=[pl.BlockSpec((H_B, HEAD_DIM, TQ), lambda i: (0, 0, i)),
                   pl.BlockSpec((G_B, NSB, TQ), lambda i: (0, 0, i))],
        compiler_params=_cp("parallel"),
        name="nsa_cmp_select",
    )(q_t, kc, vc_t, m_t, gates_t)


def _nsa_win_body(q_ref, *rest, TQ, nprev):
    k_refs = rest[:nprev + 1]
    v_refs = rest[nprev + 1:2 * nprev + 2]
    g_ref, o_ref = rest[2 * nprev + 2], rest[2 * nprev + 3]
    qb = pl.program_id(0)
    nk = (nprev + 1) * TQ
    kpos = (qb - nprev) * TQ + lax.broadcasted_iota(jnp.int32, (nk, TQ), 0)
    tq = qb * TQ + lax.broadcasted_iota(jnp.int32, (nk, TQ), 1)
    ok = (kpos >= 0) & (kpos <= tq) & (kpos > tq - WINDOW)
    for g in range(G_B):
        ks = [k_refs[b][:, g * HEAD_DIM:(g + 1) * HEAD_DIM] for b in range(nprev + 1)]
        for hh in range(HPG_B):
            h = g * HPG_B + hh
            q = q_ref[h]
            s = jnp.concatenate([jnp.dot(k, q, preferred_element_type=F32) for k in ks], axis=0)
            s = jnp.where(ok, s, NEG)
            m = jnp.max(s, axis=0, keepdims=True)
            e = jnp.where(ok, jnp.exp2(s - m), 0.0)
            p = (e / jnp.maximum(jnp.sum(e, axis=0, keepdims=True), 1e-30)).astype(v_refs[0].dtype)
            o = jnp.zeros((HEAD_DIM, TQ), F32)
            for b in range(nprev + 1):
                o = o + jnp.dot(v_refs[b][g], p[b * TQ:(b + 1) * TQ, :], preferred_element_type=F32)
            o_ref[h] = o * g_ref[2 * H_B + h:2 * H_B + h + 1, :]


def _nsa_win(q_t, kvb, vw_t, gates_t, S):
    TQ = min(TQ_ATT, S)
    nprev = WINDOW // TQ
    cw = G_B * HEAD_DIM
    kcol = (2 * 2 * G_B * HEAD_DIM) // cw
    k_specs = [pl.BlockSpec((TQ, cw), lambda i, d=d: (jnp.maximum(i - d, 0), kcol)) for d in range(nprev, -1, -1)]
    v_specs = [pl.BlockSpec((G_B, HEAD_DIM, TQ), lambda i, d=d: (0, 0, jnp.maximum(i - d, 0))) for d in range(nprev, -1, -1)]
    return pl.pallas_call(
        functools.partial(_nsa_win_body, TQ=TQ, nprev=nprev),
        out_shape=jax.ShapeDtypeStruct((H_B, HEAD_DIM, S), F32),
        grid=(S // TQ,),
        in_specs=[pl.BlockSpec((H_B, HEAD_DIM, TQ), lambda i: (0, 0, i))] + k_specs + v_specs
                 + [pl.BlockSpec((N_GATE, TQ), lambda i: (0, i))],
        out_specs=pl.BlockSpec((H_B, HEAD_DIM, TQ), lambda i: (0, 0, i)),
        compiler_params=_cp("parallel"),
        name="nsa_window",
    )(q_t, *([kvb] * (nprev + 1)), *([vw_t] * (nprev + 1)), gates_t)


def _nsa_sel_body(q_ref, k_ref, v_ref, sb_ref, g_ref, oc_ref, ow_ref, o_ref, m_ref, l_ref, acc_ref,
                  s_ref, p_ref, bias_ref, *, TQ, TK, nkt):
    qb, kt = pl.program_id(0), pl.program_id(1)
    last = ((qb + 1) * TQ - 1) // TK
    bpt = TK // SLC_BLOCK

    @pl.when(kt == 0)
    def _():
        _flash_init(m_ref, l_ref, acc_ref)

    @pl.when(kt <= last)
    def _():
        kpos = kt * TK + lax.broadcasted_iota(jnp.int32, (TK, TQ), 0)
        tq = qb * TQ + lax.broadcasted_iota(jnp.int32, (TK, TQ), 1)
        causal = kpos <= tq
        b0 = pl.multiple_of(kt * bpt, bpt)
        for g in range(G_B):
            bb = sb_ref[g, pl.ds(b0, bpt), :]
            btok = jnp.concatenate(
                [jnp.broadcast_to(bb[r:r + 1, :], (SLC_BLOCK, TQ)) for r in range(bpt)], axis=0)
            bias_ref[g] = jnp.where(causal, btok, NEG)

        def score(h):
            g = h // HPG_B
            k = k_ref[:, g * HEAD_DIM:(g + 1) * HEAD_DIM]
            return jnp.dot(k, q_ref[h], preferred_element_type=F32) + bias_ref[g]

        _flash_heads(H_B, score, lambda h: v_ref[h // HPG_B], m_ref, l_ref, acc_ref, s_ref, p_ref)

    @pl.when(kt == nkt - 1)
    def _():
        for h in range(H_B):
            o = acc_ref[h] / l_ref[h] * g_ref[H_B + h:H_B + h + 1, :] + oc_ref[h] + ow_ref[h]
            o_ref[h] = o.astype(o_ref.dtype)


def _nsa_sel(q_t, kvb, vs_t, sel_bias, gates_t, oc_t, ow_t, S):
    TQ, TK = min(TQ_ATT, S), min(TK_ATT, S)
    nqb, nkt = S // TQ, S // TK
    NSB = sel_bias.shape[1]
    cw = G_B * HEAD_DIM
    kcol = (1 * 2 * G_B * HEAD_DIM) // cw

    def kmap(qb, kt):
        return jnp.minimum(kt, ((qb + 1) * TQ - 1) // TK)

    hblk = pl.BlockSpec((H_B, HEAD_DIM, TQ), lambda qb, kt: (0, 0, qb))
    return pl.pallas_call(
        functools.partial(_nsa_sel_body, TQ=TQ, TK=TK, nkt=nkt),
        out_shape=jax.ShapeDtypeStruct((H_B, HEAD_DIM, S), CDT),
        grid=(nqb, nkt),
        in_specs=[hblk,
                  pl.BlockSpec((TK, cw), lambda qb, kt: (kmap(qb, kt), kcol)),
                  pl.BlockSpec((G_B, HEAD_DIM, TK), lambda qb, kt: (0, 0, kmap(qb, kt))),
                  pl.BlockSpec((G_B, NSB, TQ), lambda qb, kt: (0, 0, qb)),
                  pl.BlockSpec((N_GATE, TQ), lambda qb, kt: (0, qb)),
                  hblk, hblk],
        out_specs=hblk,
        scratch_shapes=_flash_scratch(H_B, HEAD_DIM, TQ, TK) + [pltpu.VMEM((G_B, TK, TQ), F32)],
        compiler_params=_cp("parallel", "arbitrary"),
        name="nsa_select_attn",
    )(q_t, kvb, vs_t, sel_bias, gates_t, oc_t, ow_t)


def _rot_cols(w, half, pad):
    parts = [-w[..., half:2 * half], w[..., :half]]
    if pad:
        parts.append(jnp.zeros(w.shape[:-1] + (pad,), w.dtype))
    return jnp.concatenate(parts, axis=-1)


def _prep_mixer_weights(w_in, w_uq, w_iq, w_uk, w_uv, gate_b):
    D = w_in.shape[0]
    cuts = np.cumsum(IN_SPLITS)[:-1].tolist()
    c_q, c_kv, k_r, k_idx, w_idx, q_b, kv_b, gate, q_m = jnp.split(w_in, cuts, axis=1)
    z = lambda n: jnp.zeros((D, n), w_in.dtype)
    w_pa = jnp.concatenate([
        c_q, c_kv, k_r, k_idx,
        _rot_cols(k_r, D_ROPE // 2, 0), _rot_cols(k_idx, D_IDX_ROPE // 2, D_IDX - D_IDX_ROPE),
        w_idx, z(LANE - H_IDX), gate, z(LANE - N_GATE)], axis=1).astype(CDT)
    w_pq = jnp.concatenate([q_b, q_m], axis=1).astype(CDT)
    w_pkv = kv_b.astype(CDT)
    gb = jnp.concatenate([gate_b, jnp.zeros((LANE - N_GATE,), F32)]).reshape(1, LANE)

    w3 = w_uq.reshape(Q_LORA, H_A, D_NOPE + D_ROPE)
    nope, pe = w3[..., :D_NOPE], w3[..., D_NOPE:]
    zq = jnp.zeros((Q_LORA, H_A, LANE - D_ROPE), w_uq.dtype)
    main = jnp.concatenate([nope, pe, zq], axis=-1).reshape(Q_LORA, H_A * 2 * LANE)
    rot = _rot_cols(pe, D_ROPE // 2, LANE - D_ROPE).reshape(Q_LORA, H_A * LANE)
    w_q = jnp.concatenate([main, rot], axis=1).astype(CDT)

    wi3 = w_iq.reshape(Q_LORA, H_IDX, D_IDX)
    roti = _rot_cols(wi3, D_IDX_ROPE // 2, D_IDX - D_IDX_ROPE).reshape(Q_LORA, H_IDX * D_IDX)
    w_qi = jnp.concatenate([w_iq, roti], axis=1).astype(CDT)

    w_kv = jnp.concatenate([w_uk.reshape(KV_LORA, W_A), w_uv.reshape(KV_LORA, W_A)], axis=1).astype(CDT)
    return w_pa, w_pq, w_pkv, gb, w_q, w_qi, w_kv


def _cmp_to_slc_t(S):
    n_c = S // CMP_STRIDE
    n_sb = S // SLC_BLOCK
    cs = np.arange(n_c) * CMP_STRIDE
    ss = np.arange(n_sb) * SLC_BLOCK
    m = (cs[None, :] < ss[:, None] + SLC_BLOCK) & (cs[None, :] + CMP_LEN > ss[:, None])
    m[:, (S - CMP_LEN) // CMP_STRIDE + 1:] = False
    return jnp.asarray(m.astype(np.float32)).astype(CDT)


def _mixer(x, xc, mem_c, tabs, lw, S):
    (w_pa, w_pq, w_pkv, gb, w_q, w_qi, w_kv, g_cq, g_ckv, cmp_w1, cmp_b1, cmp_w2, cmp_b2, cmp_pe,
     w_mem_kv, w_out) = lw
    rowtab = lambda t: (tabs[t], "row")

    cq, ckv, kk, wi, gates = _mm_rows(
        xc, w_pa, _epi_dsa_prep,
        [(g_cq.reshape(1, -1), "full"), (g_ckv.reshape(1, -1), "full"), (gb, "full"), rowtab(0), rowtab(1)],
        [(Q_LORA, CDT), (KV_LORA, CDT), (LANE, CDT), (LANE, F32), (LANE, F32)], "proj_dsa")
    q_b, q_m = _mm_rows(xc, w_pq, _epi_q_nsa_mem, [rowtab(6), rowtab(7)],
                        [(W_B, CDT), (W_M, CDT)], "proj_q")
    kvb, = _mm_rows(xc, w_pkv, _epi_kv_nsa, [rowtab(6), rowtab(7)], [(N_KVB, CDT)], "proj_kv")

    q_ext, = _mm_rows(cq, w_q, _epi_dsa_q, [rowtab(2), rowtab(3)], [(H_A * 2 * LANE, CDT)], "dsa_q")
    q_idx, = _mm_rows(cq, w_qi, _epi_idx_q, [rowtab(4), rowtab(5)], [(H_IDX * D_IDX, CDT)], "dsa_qi")
    kv_a = _mm(ckv, w_kv, CDT, "dsa_kv")
    k_pe, k_i = kk[:, :D_ROPE], kk[:, D_ROPE:]
    k_nope = kv_a[:, :W_A].reshape(S, H_A, D_NOPE).transpose(1, 0, 2)
    k_ext = jnp.concatenate([
        k_nope, jnp.broadcast_to(k_pe[None], (H_A, S, D_ROPE)),
        jnp.zeros((H_A, S, LANE - D_ROPE), CDT)], axis=-1)
    va_t = kv_a[:, W_A:].reshape(S, H_A, HEAD_DIM).transpose(1, 2, 0)
    qa_t = q_ext.reshape(S, H_A, 2 * LANE).transpose(1, 2, 0)
    nqi = S // TQ_IDX
    qi_t = q_idx.reshape(nqi, TQ_IDX, H_IDX, D_IDX).transpose(0, 3, 2, 1).reshape(nqi, D_IDX, H_IDX * TQ_IDX)
    w_t = wi[:, :H_IDX].reshape(nqi, TQ_IDX, H_IDX).transpose(0, 2, 1)
    topk = min(TOPK_MAX, S // 4)
    dsa_bias = _indexer(qi_t, w_t, k_i, S, topk)
    oa_t = _dsa_attn(qa_t, k_ext, va_t, dsa_bias, S)
    o_a = oa_t.transpose(2, 0, 1).reshape(S, W_A)

    gates_t = gates[:, :N_GATE].T
    qb_t = q_b.reshape(S, H_B, HEAD_DIM).transpose(1, 2, 0)
    cw = G_B * HEAD_DIM
    nch = S // CMP_STRIDE
    ld = CMP_LEN * HEAD_DIM

    def cmp_blocks(t):
        ch = t.reshape(nch, CMP_STRIDE, G_B, HEAD_DIM).transpose(2, 0, 1, 3).reshape(G_B, nch, CMP_STRIDE * HEAD_DIM)
        nxt = jnp.concatenate([ch[:, 1:], jnp.zeros((G_B, 1, CMP_STRIDE * HEAD_DIM), ch.dtype)], axis=1)
        return jnp.concatenate([ch, nxt], axis=-1)

    blocks = jnp.stack([cmp_blocks(kvb[:, 0:cw]), cmp_blocks(kvb[:, cw:2 * cw])])
    cmp_out = _compress(blocks, cmp_pe.reshape(2, 1, ld), cmp_w1.astype(CDT), cmp_b1.reshape(2, 1, CMP_HID),
                        cmp_w2.astype(CDT), cmp_b2.reshape(2, 1, HEAD_DIM))
    kc = cmp_out[0]
    vc_t = cmp_out[1].transpose(0, 2, 1)
    n_sb = S // SLC_BLOCK
    oc_t, sel_bias = _nsa_cmp(qb_t, kc, vc_t, _cmp_to_slc_t(S), gates_t, S, min(N_SLC, n_sb))
    vs_t = kvb[:, 3 * cw:4 * cw].reshape(S, G_B, HEAD_DIM).transpose(1, 2, 0)
    vw_t = kvb[:, 5 * cw:6 * cw].reshape(S, G_B, HEAD_DIM).transpose(1, 2, 0)
    ow_t = _nsa_win(qb_t, kvb, vw_t, gates_t, S)
    ob_t = _nsa_sel(qb_t, kvb, vs_t, sel_bias, gates_t, oc_t, ow_t, S)
    o_b = ob_t.transpose(2, 0, 1).reshape(S, W_B)

    n_mem = mem_c.shape[0]
    mem_kv = _mm(mem_c, w_mem_kv.astype(CDT), CDT, "mem_kv")
    mk_t = mem_kv[:, :W_M].reshape(n_mem, H_M, D_M).transpose(1, 2, 0)
    mv = mem_kv[:, W_M:].reshape(n_mem, H_M, D_M).transpose(1, 0, 2)
    o_m = _mem_attn(q_m, mk_t, mv)

    mix = _mm(jnp.concatenate([o_a, o_b, o_m], axis=1), w_out.astype(CDT), F32, "mix_out")
    return mix


def kernel(x, mem, positions, w_in, g_cq, g_ckv, w_uq, w_iq, w_uk, w_uv, cmp_w1, cmp_b1, cmp_w2, cmp_b2,
           cmp_pe, gate_b, w_mem_kv, w_out, ln1_g, ln1_b, ln2_g, ln2_b, ffn_wg, ffn_wu, ffn_wd,
           router_w, moe_wg, moe_wu, moe_wd):
    B, S, D = x.shape
    assert B == 1 and S % max(TK_ATT, TK_IDX) == 0 and WINDOW % TQ_ATT == 0
    depth = w_in.shape[0]
    alpha = (2 * depth) ** 0.25
    xs = x[0]
    mem_c = mem[0].astype(CDT)
    tabs = _rope_tables(positions)
    for i in range(depth):
        lw = _prep_mixer_weights(w_in[i], w_uq[i], w_iq[i], w_uk[i], w_uv[i], gate_b[i]) + (
            g_cq[i], g_ckv[i], cmp_w1[i], cmp_b1[i], cmp_w2[i], cmp_b2[i], cmp_pe[i], w_mem_kv[i], w_out[i])
        mix = _mixer(xs, xs.astype(CDT), mem_c, tabs, lw, S)
        xs = _res_ln(xs, mix, ln1_g[i], ln1_b[i], alpha)
        xc = xs.astype(CDT)
        j = i // 2
        if i % 2 == 0:
            h = _swiglu_up(xc, ffn_wg[j].astype(CDT), ffn_wu[j].astype(CDT), "ffn_up")
            f = _mm(h, ffn_wd[j].astype(CDT), F32, "ffn_down")
        else:
            n_exp, _, dfe = moe_wg[j].shape[0], moe_wg[j].shape[1], moe_wg[j].shape[2]
            rw = jnp.concatenate([router_w[j], jnp.zeros((D, LANE - n_exp), F32)], axis=1).astype(CDT)
            gate, = _mm_rows(xc, rw, _make_epi_router(n_exp), [], [(LANE, F32)], "moe_router")
            scale = gate[:, :n_exp].T.reshape(n_exp, S, 1)
            wg = moe_wg[j].transpose(1, 0, 2).reshape(D, n_exp * dfe).astype(CDT)
            wu = moe_wu[j].transpose(1, 0, 2).reshape(D, n_exp * dfe).astype(CDT)
            h = _swiglu_up(xc, wg, wu, "moe_up", scale=scale, n_per_scale=dfe)
            f = _mm(h, moe_wd[j].reshape(n_exp * dfe, D).astype(CDT), F32, "moe_down")
        xs = _res_ln(xs, f, ln2_g[i], ln2_b[i], alpha)
    return xs[None]
```

```python
import functools
import math

import numpy as np
import jax
import jax.numpy as jnp
from jax import lax
from jax.experimental import pallas as pl
from jax.experimental.pallas import tpu as pltpu

F32 = jnp.float32
CDT = jnp.bfloat16

HEAD_DIM = 128
H_A = 12
H_B = 12
H_M = 4
D_M = 256
Q_LORA = 1024
KV_LORA = 512
D_NOPE = 128
D_ROPE = 64
H_IDX = 32
D_IDX = 64
D_IDX_ROPE = 32
TOPK_MAX = 256
G_B = 4
HPG_B = H_B // G_B
CMP_LEN = 32
CMP_STRIDE = 16
CMP_HID = 256
SLC_BLOCK = 64
N_SLC = 16
WINDOW = 512
TOP_E = 2
ROPE_THETA = 10000.0
EPS = 1e-5

W_A = H_A * HEAD_DIM
W_B = H_B * HEAD_DIM
W_M = H_M * D_M
N_KVB = 3 * 2 * G_B * HEAD_DIM
N_GATE = 3 * H_B
IN_SPLITS = (Q_LORA, KV_LORA, D_ROPE, D_IDX, H_IDX, W_B, N_KVB, N_GATE, W_M)

LANE = 128
NEG = -1e30
LOG2E = math.log2(math.e)
INT_MIN = -2147483648
VMEM_LIMIT = 48 * 1024 * 1024

TQ_IDX = 128
TK_IDX = 512
TQ_ATT = 256
TK_ATT = 1024


def _cp(*sem):
    return pltpu.CompilerParams(dimension_semantics=sem, vmem_limit_bytes=VMEM_LIMIT)


def _pick(n, cands):
    for c in cands:
        if c <= n and n % c == 0:
            return c
    return n


def _mm_body(x_ref, w_ref, o_ref, *scratch, nk):
    part = jnp.dot(x_ref[...], w_ref[...], preferred_element_type=F32)
    if nk == 1:
        o_ref[...] = part.astype(o_ref.dtype)
        return
    acc_ref, = scratch
    k = pl.program_id(2)

    @pl.when(k == 0)
    def _():
        acc_ref[...] = part

    @pl.when(k > 0)
    def _():
        acc_ref[...] += part

    @pl.when(k == nk - 1)
    def _():
        o_ref[...] = acc_ref[...].astype(o_ref.dtype)


def _mm(x, w, out_dtype, name):
    M, K = x.shape
    N = w.shape[1]
    tm = _pick(M, (512, 256, 128))
    tn = _pick(N, (1024, 512, 256, 128))
    tk = K
    for parts in (1, 2, 4, 8, 16):
        if K % parts == 0 and (K // parts) % LANE == 0 and (K // parts) * max(tm, tn) * 2 <= 12 * 2 ** 20:
            tk = K // parts
            break
    nk = K // tk
    scratch = [pltpu.VMEM((tm, tn), F32)] if nk > 1 else []
    return pl.pallas_call(
        functools.partial(_mm_body, nk=nk),
        out_shape=jax.ShapeDtypeStruct((M, N), out_dtype),
        grid=(M // tm, N // tn, nk),
        in_specs=[pl.BlockSpec((tm, tk), lambda i, j, k: (i, k)),
                  pl.BlockSpec((tk, tn), lambda i, j, k: (k, j))],
        out_specs=pl.BlockSpec((tm, tn), lambda i, j, k: (i, j)),
        scratch_shapes=scratch,
        compiler_params=_cp("parallel", "parallel", "arbitrary"),
        name=name,
    )(x, w)


def _mm_rows_body(*refs, nk, n_aux, n_out, epi):
    x_ref, w_ref = refs[0], refs[1]
    aux = refs[2:2 + n_aux]
    outs = refs[2 + n_aux:2 + n_aux + n_out]
    acc_ref = refs[2 + n_aux + n_out]
    k = pl.program_id(1)
    N = w_ref.shape[1]
    cw = _pick(N, (512, 256, 128))
    x = x_ref[...]

    @pl.when(k == 0)
    def _():
        for c in range(0, N, cw):
            acc_ref[:, c:c + cw] = jnp.dot(x, w_ref[:, c:c + cw], preferred_element_type=F32)

    @pl.when(k > 0)
    def _():
        for c in range(0, N, cw):
            acc_ref[:, c:c + cw] += jnp.dot(x, w_ref[:, c:c + cw], preferred_element_type=F32)

    @pl.when(k == nk - 1)
    def _():
        epi(acc_ref, aux, outs)


def _mm_rows(x, w, epi, aux, outs, name, tm=None):
    M, K = x.shape
    N = w.shape[1]
    tm = tm or _pick(M, (512, 256, 128))
    tk = _pick(K, (1024, 512, 256, 128))
    nk = K // tk
    in_specs = [pl.BlockSpec((tm, tk), lambda i, k: (i, k)),
                pl.BlockSpec((tk, N), lambda i, k: (k, 0))]
    for a, kind in aux:
        if kind == "row":
            in_specs.append(pl.BlockSpec((tm, a.shape[1]), lambda i, k: (i, 0)))
        else:
            in_specs.append(pl.BlockSpec(a.shape, lambda i, k, nd=a.ndim: (0,) * nd))
    res = pl.pallas_call(
        functools.partial(_mm_rows_body, nk=nk, n_aux=len(aux), n_out=len(outs), epi=epi),
        out_shape=[jax.ShapeDtypeStruct((M, c), dt) for c, dt in outs],
        grid=(M // tm, nk),
        in_specs=in_specs,
        out_specs=[pl.BlockSpec((tm, c), lambda i, k: (i, 0)) for c, _ in outs],
        scratch_shapes=[pltpu.VMEM((tm, N), F32)],
        compiler_params=_cp("parallel", "arbitrary"),
        name=name,
    )(x, w, *[a for a, _ in aux])
    return res


def _epi_plain(acc_ref, aux, outs):
    outs[0][...] = acc_ref[...].astype(outs[0].dtype)


def _rope_tab_body(pos_ref, inv_ref, sgn_ref, o_ref):
    pos = pos_ref[...].astype(F32)
    nt = inv_ref.shape[0]
    for t in range(nt):
        ang = pos * inv_ref[t:t + 1, :]
        o_ref[2 * t] = jnp.cos(ang)
        o_ref[2 * t + 1] = jnp.sin(ang) * sgn_ref[t:t + 1, :]


def _inv_freq(d):
    return jnp.power(ROPE_THETA, -jnp.arange(0, d, 2, dtype=F32) / d)


def _rope_tables(positions):
    S = positions.shape[1]
    i32, i16, i64 = _inv_freq(D_ROPE), _inv_freq(D_IDX_ROPE), _inv_freq(HEAD_DIM)
    z32, z64 = jnp.zeros((32,), F32), jnp.zeros((64,), F32)
    inv = jnp.stack([
        jnp.concatenate([i32, i32, i16, i16, z32]),
        jnp.concatenate([i32, i32, z64]),
        jnp.concatenate([i16, i16, z32, i16, i16, z32]),
        jnp.concatenate([i64, i64]),
    ])
    sgn = jnp.ones((4, LANE), F32).at[3, :64].set(-1.0)
    tm = _pick(S, (512, 256, 128))
    return pl.pallas_call(
        _rope_tab_body,
        out_shape=jax.ShapeDtypeStruct((8, S, LANE), F32),
        grid=(S // tm,),
        in_specs=[pl.BlockSpec((tm, 1), lambda i: (i, 0)),
                  pl.BlockSpec((4, LANE), lambda i: (0, 0)),
                  pl.BlockSpec((4, LANE), lambda i: (0, 0))],
        out_specs=pl.BlockSpec((8, tm, LANE), lambda i: (0, i, 0)),
        compiler_params=_cp("parallel"),
        name="rope_tables",
    )(positions.reshape(S, 1), inv, sgn)


N_PA = Q_LORA + KV_LORA + 4 * LANE


def _rms(v, g):
    return v * lax.rsqrt(jnp.mean(v * v, axis=-1, keepdims=True) + EPS) * g


def _epi_dsa_prep(acc_ref, aux, outs):
    gq_ref, gkv_ref, gb_ref, tc_ref, ts_ref = aux
    cq_ref, ckv_ref, kk_ref, wi_ref, gate_ref = outs
    o = Q_LORA + KV_LORA
    cq_ref[...] = _rms(acc_ref[:, 0:Q_LORA], gq_ref[...]).astype(cq_ref.dtype)
    ckv_ref[...] = _rms(acc_ref[:, Q_LORA:o], gkv_ref[...]).astype(ckv_ref.dtype)
    kk = acc_ref[:, o:o + LANE] * tc_ref[...] + acc_ref[:, o + LANE:o + 2 * LANE] * ts_ref[...]
    kk_ref[...] = kk.astype(kk_ref.dtype)
    wi_ref[...] = acc_ref[:, o + 2 * LANE:o + 3 * LANE] * (H_IDX ** -0.5 * D_IDX ** -0.5)
    gate_ref[...] = jax.nn.sigmoid(acc_ref[:, o + 3 * LANE:o + 4 * LANE] + gb_ref[...])


def _epi_q_nsa_mem(acc_ref, aux, outs):
    tc_ref, ts_ref = aux
    qb_ref, qm_ref = outs
    tc, ts = tc_ref[...], ts_ref[...]
    for h in range(H_B):
        t = acc_ref[:, h * HEAD_DIM:(h + 1) * HEAD_DIM]
        r = (t * tc + pltpu.roll(t, HEAD_DIM // 2, 1) * ts) * (HEAD_DIM ** -0.5 * LOG2E)
        qb_ref[:, h * HEAD_DIM:(h + 1) * HEAD_DIM] = r.astype(qb_ref.dtype)
    qm_ref[...] = (acc_ref[:, W_B:W_B + W_M] * (D_M ** -0.5 * LOG2E)).astype(qm_ref.dtype)


def _epi_kv_nsa(acc_ref, aux, outs):
    tc_ref, ts_ref = aux
    o_ref, = outs
    tc, ts = tc_ref[...], ts_ref[...]
    for br in range(3):
        for kv in range(2):
            for g in range(G_B):
                c = ((br * 2 + kv) * G_B + g) * HEAD_DIM
                t = acc_ref[:, c:c + HEAD_DIM]
                if kv == 0:
                    t = t * tc + pltpu.roll(t, HEAD_DIM // 2, 1) * ts
                o_ref[:, c:c + HEAD_DIM] = t.astype(o_ref.dtype)


def _epi_dsa_q(acc_ref, aux, outs):
    tc_ref, ts_ref = aux
    o_ref, = outs
    tc, ts = tc_ref[...], ts_ref[...]
    sc = (D_NOPE + D_ROPE) ** -0.5 * LOG2E
    W = 2 * LANE
    for h in range(H_A):
        o_ref[:, h * W:h * W + LANE] = (acc_ref[:, h * W:h * W + LANE] * sc).astype(o_ref.dtype)
        pe = acc_ref[:, h * W + LANE:(h + 1) * W] * tc + acc_ref[:, H_A * W + h * LANE:H_A * W + (h + 1) * LANE] * ts
        o_ref[:, h * W + LANE:(h + 1) * W] = (pe * sc).astype(o_ref.dtype)


def _epi_idx_q(acc_ref, aux, outs):
    tc_ref, ts_ref = aux
    o_ref, = outs
    tc, ts = tc_ref[...], ts_ref[...]
    n = H_IDX * D_IDX
    for c in range(n // LANE):
        r = acc_ref[:, c * LANE:(c + 1) * LANE] * tc + acc_ref[:, n + c * LANE:n + (c + 1) * LANE] * ts
        o_ref[:, c * LANE:(c + 1) * LANE] = r.astype(o_ref.dtype)


def _res_ln_body(x_ref, f_ref, g_ref, b_ref, o_ref, *, alpha):
    v = alpha * x_ref[...] + f_ref[...].astype(F32)
    mu = jnp.mean(v, axis=-1, keepdims=True)
    d = v - mu
    var = jnp.mean(d * d, axis=-1, keepdims=True)
    o_ref[...] = d * lax.rsqrt(var + EPS) * g_ref[...] + b_ref[...]


def _res_ln(x, f, g, b, alpha):
    S, D = x.shape
    tm = _pick(S, (256, 128))
    row = pl.BlockSpec((tm, D), lambda i: (i, 0))
    vec = pl.BlockSpec((1, D), lambda i: (0, 0))
    return pl.pallas_call(
        functools.partial(_res_ln_body, alpha=alpha),
        out_shape=jax.ShapeDtypeStruct((S, D), F32),
        grid=(S // tm,),
        in_specs=[row, row, vec, vec],
        out_specs=row,
        compiler_params=_cp("parallel"),
        name="res_ln",
    )(x, f, g.reshape(1, D), b.reshape(1, D))


def _swiglu_up_body(x_ref, wg_ref, wu_ref, *rest, scaled):
    if scaled:
        s_ref, o_ref = rest
    else:
        o_ref, = rest
    x = x_ref[...]
    g = jnp.dot(x, wg_ref[...], preferred_element_type=F32)
    u = jnp.dot(x, wu_ref[...], preferred_element_type=F32)
    h = g * jax.nn.sigmoid(g) * u
    if scaled:
        h = h * s_ref[...]
    o_ref[...] = h.astype(o_ref.dtype)


def _swiglu_up(x, wg, wu, name, scale=None, n_per_scale=None):
    M, K = x.shape
    N = wg.shape[1]
    tm = _pick(M, (512, 256, 128))
    if scale is None:
        tn = _pick(N, (512, 256, 128))
    else:
        tn = _pick(n_per_scale, (512, 256, 128))
    in_specs = [pl.BlockSpec((tm, K), lambda i, j: (i, 0)),
                pl.BlockSpec((K, tn), lambda i, j: (0, j)),
                pl.BlockSpec((K, tn), lambda i, j: (0, j))]
    args = [x, wg, wu]
    if scale is not None:
        per = n_per_scale // tn
        in_specs.append(pl.BlockSpec((None, tm, 1), lambda i, j: (j // per, i, 0)))
        args.append(scale)
    return pl.pallas_call(
        functools.partial(_swiglu_up_body, scaled=scale is not None),
        out_shape=jax.ShapeDtypeStruct((M, N), CDT),
        grid=(M // tm, N // tn),
        in_specs=in_specs,
        out_specs=pl.BlockSpec((tm, tn), lambda i, j: (i, j)),
        compiler_params=_cp("parallel", "parallel"),
        name=name,
    )(*args)


def _make_epi_router(n_exp):
    def epi(acc_ref, aux, outs):
        o_ref, = outs
        lg = acc_ref[...]
        lane = lax.broadcasted_iota(jnp.int32, lg.shape, 1)
        lg = jnp.where(lane < n_exp, lg, -jnp.inf)
        m1 = jnp.max(lg, axis=-1, keepdims=True)
        i1 = jnp.min(jnp.where(lg == m1, lane, LANE), axis=-1, keepdims=True)
        l2 = jnp.where(lane == i1, -jnp.inf, lg)
        m2 = jnp.max(l2, axis=-1, keepdims=True)
        i2 = jnp.min(jnp.where(l2 == m2, lane, LANE), axis=-1, keepdims=True)
        e2 = jnp.exp(m2 - m1)
        den = 1.0 + e2
        o_ref[...] = jnp.where(lane == i1, 1.0 / den, 0.0) + jnp.where(lane == i2, e2 / den, 0.0)
    return epi


def _mem_attn_body(q_ref, kt_ref, v_ref, o_ref):
    for h in range(H_M):
        q = q_ref[:, h * D_M:(h + 1) * D_M]
        s = jnp.dot(q, kt_ref[h], preferred_element_type=F32)
        m = jnp.max(s, axis=-1, keepdims=True)
        e = jnp.exp2(s - m)
        p = e / jnp.sum(e, axis=-1, keepdims=True)
        o = jnp.dot(p.astype(v_ref.dtype), v_ref[h], preferred_element_type=F32)
        o_ref[:, h * D_M:(h + 1) * D_M] = o.astype(o_ref.dtype)


def _mem_attn(q_m, k_t, v):
    S = q_m.shape[0]
    n_mem = v.shape[1]
    tm = _pick(S, (512, 256, 128))
    return pl.pallas_call(
        _mem_attn_body,
        out_shape=jax.ShapeDtypeStruct((S, W_M), CDT),
        grid=(S // tm,),
        in_specs=[pl.BlockSpec((tm, W_M), lambda i: (i, 0)),
                  pl.BlockSpec((H_M, D_M, n_mem), lambda i: (0, 0, 0)),
                  pl.BlockSpec((H_M, n_mem, D_M), lambda i: (0, 0, 0))],
        out_specs=pl.BlockSpec((tm, W_M), lambda i: (i, 0)),
        compiler_params=_cp("parallel"),
        name="mem_attn",
    )(q_m, k_t, v)


def _float_key(x):
    b = pltpu.bitcast(x, jnp.int32)
    return jnp.where(b >= 0, b, b ^ jnp.int32(0x7FFFFFFF))


def _indexer_body(qi_ref, w_ref, ki_ref, o_ref, keys_ref, *, S, topk, TQ, TK, HC):
    j = pl.program_id(0)
    n_valid = (j + 1) * TQ
    nkt = (n_valid + TK - 1) // TK

    @pl.when(j == 0)
    def _():
        keys_ref[...] = jnp.full(keys_ref.shape, INT_MIN, jnp.int32)

    qpos = j * TQ + lax.broadcasted_iota(jnp.int32, (TK, TQ), 1)
    krow = lax.broadcasted_iota(jnp.int32, (TK, TQ), 0)

    def score_tile(kt, carry):
        r0 = pl.multiple_of(kt * TK, TK)
        k = ki_ref[pl.ds(r0, TK), :]
        acc = jnp.zeros((TK, TQ), F32)
        for hc in range(H_IDX // HC):
            lg = jnp.dot(k, qi_ref[0, :, hc * HC * TQ:(hc + 1) * HC * TQ],
                         preferred_element_type=F32)
            for hh in range(HC):
                h = hc * HC + hh
                acc = acc + jnp.maximum(lg[:, hh * TQ:(hh + 1) * TQ], 0.0) * w_ref[0, h:h + 1, :]
        key = jnp.where(krow + r0 <= qpos, _float_key(acc), INT_MIN)
        keys_ref[pl.ds(r0, TK), :] = key
        return carry

    lax.fori_loop(0, nkt, score_tile, 0)

    RB = 64

    def count_where(pred):
        def body(kt, c):
            r0 = pl.multiple_of(kt * TK, TK)
            hit = jnp.where(pred(keys_ref[pl.ds(r0, TK), :], r0), 1.0, 0.0)
            return c + jnp.sum(hit.reshape(TK // RB, RB, TQ), axis=0)
        c = lax.fori_loop(0, nkt, body, jnp.zeros((RB, TQ), F32))
        return jnp.sum(c, axis=0, keepdims=True)

    def count_ge(cand):
        return count_where(lambda key, r0: key >= cand)

    BPG = 4

    def bit_group(carry):
        g, prefix, cnt_p = carry
        for i in range(BPG):
            bv = jnp.left_shift(jnp.int32(1), 31 - (g * BPG + i))
            cand_u = prefix | bv
            cnt = count_ge(cand_u ^ jnp.int32(INT_MIN))
            ok = cnt >= float(topk)
            prefix = jnp.where(ok, cand_u, prefix)
            cnt_p = jnp.where(ok, cnt, cnt_p)
        return g + 1, prefix, cnt_p

    def undecided(carry):
        g, _, cnt_p = carry
        return (g < 32 // BPG) & (jnp.max(jnp.where(cnt_p == float(topk), 0.0, 1.0)) > 0.0)

    start = (jnp.int32(0), jnp.zeros((1, TQ), jnp.int32), jnp.full((1, TQ), nkt * TK, jnp.int32).astype(F32))
    _, prefix, cnt_ge = lax.while_loop(undecided, bit_group, start)
    thr = prefix ^ jnp.int32(INT_MIN)
    thr_eff = jnp.maximum(thr, jnp.int32(INT_MIN + 1))
    has_tie = jnp.max(jnp.where((cnt_ge > float(topk)) & (thr != jnp.int32(INT_MIN)), 1.0, 0.0)) > 0.0

    def write_rows(select):
        def write_tile(kt, carry):
            r0 = pl.multiple_of(kt * TK, TK)
            sel = select(keys_ref[pl.ds(r0, TK), :], r0)
            o_ref[pl.ds(r0, TK), :] = jnp.where(sel, 0.0, NEG).astype(o_ref.dtype)
            return carry
        lax.fori_loop(0, nkt, write_tile, 0)

    @pl.when(jnp.logical_not(has_tie))
    def _():
        write_rows(lambda key, r0: key >= thr_eff)

    @pl.when(has_tie)
    def _():
        top = thr == jnp.int32(0x7FFFFFFF)
        cnt_gt = jnp.where(top, 0.0, count_ge(jnp.where(top, thr, thr + 1)))
        need = float(topk) - cnt_gt
        lo = jnp.zeros((1, TQ), jnp.int32)
        hi = jnp.full((1, TQ), S - 1, jnp.int32)
        for _ in range(max(1, int(math.ceil(math.log2(S))))):
            mid = (lo + hi) >> 1
            ok = count_where(lambda key, r0: (key == thr) & (krow + r0 <= mid)) >= need
            hi = jnp.where(ok, mid, hi)
            lo = jnp.where(ok, lo, mid + 1)
        write_rows(lambda key, r0: (key > thr_eff) | ((key == thr_eff) & (krow + r0 <= hi)))

    def write_masked(kt, carry):
        o_ref[pl.ds(pl.multiple_of(kt * TK, TK), TK), :] = jnp.full((TK, TQ), NEG, o_ref.dtype)
        return carry

    lax.fori_loop(nkt, S // TK, write_masked, 0)


def _indexer(qi_t, w_t, k_i, S, topk):
    TQ, TK = TQ_IDX, min(TK_IDX, S)
    nqb = S // TQ
    return pl.pallas_call(
        functools.partial(_indexer_body, S=S, topk=topk, TQ=TQ, TK=TK, HC=4),
        out_shape=jax.ShapeDtypeStruct((S, S), jnp.bfloat16),
        grid=(nqb,),
        in_specs=[pl.BlockSpec((1, D_IDX, H_IDX * TQ), lambda j: (j, 0, 0)),
                  pl.BlockSpec((1, H_IDX, TQ), lambda j: (j, 0, 0)),
                  pl.BlockSpec((S, D_IDX), lambda j: (0, 0))],
        out_specs=pl.BlockSpec((S, TQ), lambda j: (0, j)),
        scratch_shapes=[pltpu.VMEM((S, TQ), jnp.int32)],
        compiler_params=_cp("arbitrary"),
        name="dsa_indexer",
    )(qi_t, w_t, k_i)


DV_EXT = HEAD_DIM + 16


def _with_ones_row(v_t):
    H, dv, S = v_t.shape
    return jnp.concatenate([v_t, jnp.ones((H, 1, S), v_t.dtype), jnp.zeros((H, DV_EXT - dv - 1, S), v_t.dtype)],
                           axis=1)


def _flash_heads(n_heads, score_fn, vt_fn, m_ref, acc_ref, s_ref, p_ref):
    s_ref[0] = score_fn(0)
    for h in range(n_heads):
        b = h % 2
        if h + 1 < n_heads:
            s_ref[1 - b] = score_fn(h + 1)
        s = s_ref[b]
        m_old = m_ref[h]
        m_new = jnp.maximum(m_old, jnp.max(s, axis=0, keepdims=True))
        a = jnp.exp2(m_old - m_new)
        p_ref[b] = jnp.exp2((s - m_new).astype(p_ref.dtype))
        acc_ref[h] = a * acc_ref[h] + jnp.dot(vt_fn(h), p_ref[b], preferred_element_type=F32)
        m_ref[h] = m_new


def _flash_init(m_ref, acc_ref):
    m_ref[...] = jnp.full(m_ref.shape, NEG, F32)
    acc_ref[...] = jnp.zeros(acc_ref.shape, F32)


def _flash_out(acc_ref, h):
    return acc_ref[h, :HEAD_DIM, :] / acc_ref[h, HEAD_DIM:HEAD_DIM + 1, :]


def _flash_scratch(n_heads, TQ, TK):
    return [pltpu.VMEM((n_heads, 1, TQ), F32), pltpu.VMEM((n_heads, DV_EXT, TQ), F32),
            pltpu.VMEM((2, TK, TQ), F32), pltpu.VMEM((2, TK, TQ), CDT)]


def _dsa_attn_body(q_ref, k_ref, v_ref, b_ref, o_ref, m_ref, acc_ref, s_ref, p_ref, bias_ref,
                   *, TQ, TK, nkt):
    qb, kt = pl.program_id(0), pl.program_id(1)
    last = ((qb + 1) * TQ - 1) // TK

    @pl.when(kt == 0)
    def _():
        _flash_init(m_ref, acc_ref)

    @pl.when(kt <= last)
    def _():
        bias_ref[...] = b_ref[...].astype(F32)
        _flash_heads(
            H_A,
            lambda h: jnp.dot(k_ref[h], q_ref[h], preferred_element_type=F32) + bias_ref[...],
            lambda h: v_ref[h], m_ref, acc_ref, s_ref, p_ref)

    @pl.when(kt == nkt - 1)
    def _():
        for h in range(H_A):
            o_ref[h] = _flash_out(acc_ref, h).astype(o_ref.dtype)


def _dsa_attn(q_t, k_ext, v_t, bias, S):
    TQ, TK = min(TQ_ATT, S), min(TK_ATT, S)
    nqb, nkt = S // TQ, S // TK

    def kmap(qb, kt):
        return jnp.minimum(kt, ((qb + 1) * TQ - 1) // TK)

    return pl.pallas_call(
        functools.partial(_dsa_attn_body, TQ=TQ, TK=TK, nkt=nkt),
        out_shape=jax.ShapeDtypeStruct((H_A, HEAD_DIM, S), CDT),
        grid=(nqb, nkt),
        in_specs=[pl.BlockSpec((H_A, 2 * LANE, TQ), lambda qb, kt: (0, 0, qb)),
                  pl.BlockSpec((H_A, TK, 2 * LANE), lambda qb, kt: (0, kmap(qb, kt), 0)),
                  pl.BlockSpec((H_A, DV_EXT, TK), lambda qb, kt: (0, 0, kmap(qb, kt))),
                  pl.BlockSpec((TK, TQ), lambda qb, kt: (kmap(qb, kt), qb))],
        out_specs=pl.BlockSpec((H_A, HEAD_DIM, TQ), lambda qb, kt: (0, 0, qb)),
        scratch_shapes=_flash_scratch(H_A, TQ, TK) + [pltpu.VMEM((TK, TQ), F32)],
        compiler_params=_cp("parallel", "arbitrary"),
        name="dsa_attn",
    )(q_t, k_ext, v_t, bias)


def _compress_body(blk_ref, pe_ref, w1_ref, b1_ref, w2_ref, b2_ref, o_ref):
    tb = (blk_ref[0, 0].astype(F32) + pe_ref[0]).astype(w1_ref.dtype)
    hdn = jax.nn.gelu(jnp.dot(tb, w1_ref[0], preferred_element_type=F32) + b1_ref[0])
    out = jnp.dot(hdn.astype(w2_ref.dtype), w2_ref[0], preferred_element_type=F32) + b2_ref[0]
    o_ref[0, 0] = out.astype(o_ref.dtype)


def _compress(blocks, pe, w1, b1, w2, b2):
    _, G, NC, LD = blocks.shape
    hid, D = w2.shape[1], w2.shape[2]
    return pl.pallas_call(
        _compress_body,
        out_shape=jax.ShapeDtypeStruct((2, G, NC, D), CDT),
        grid=(2, G),
        in_specs=[pl.BlockSpec((1, 1, NC, LD), lambda j, g: (j, g, 0, 0)),
                  pl.BlockSpec((1, 1, LD), lambda j, g: (j, 0, 0)),
                  pl.BlockSpec((1, LD, hid), lambda j, g: (j, 0, 0)),
                  pl.BlockSpec((1, 1, hid), lambda j, g: (j, 0, 0)),
                  pl.BlockSpec((1, hid, D), lambda j, g: (j, 0, 0)),
                  pl.BlockSpec((1, 1, D), lambda j, g: (j, 0, 0))],
        out_specs=pl.BlockSpec((1, 1, NC, D), lambda j, g: (j, g, 0, 0)),
        compiler_params=_cp("parallel", "parallel"),
        name="nsa_compress",
    )(blocks, pe, w1, b1, w2, b2)


def _nsa_cmp_body(q_ref, kc_ref, vct_ref, mt_ref, g_ref, oc_ref, sb_ref, *, TQ, NC, NSB, n_sel):
    qb = pl.program_id(0)
    tq = qb * TQ + lax.broadcasted_iota(jnp.int32, (NC, TQ), 1)
    cmp_end = lax.broadcasted_iota(jnp.int32, (NC, TQ), 0) * CMP_STRIDE + (CMP_LEN - 1)
    vis = cmp_end <= tq
    blk = lax.broadcasted_iota(jnp.int32, (NSB, TQ), 0)
    tqs = qb * TQ + lax.broadcasted_iota(jnp.int32, (NSB, TQ), 1)
    cur = jnp.right_shift(tqs, SLC_BLOCK.bit_length() - 1)
    forced = (blk == 0) | (blk == cur) | (blk == cur - 1)
    causal_b = blk * SLC_BLOCK <= tqs
    blk_f = blk.astype(F32)
    for g in range(G_B):
        imp = jnp.zeros((NSB, TQ), F32)
        for hh in range(HPG_B):
            h = g * HPG_B + hh
            s = jnp.dot(kc_ref[g], q_ref[h], preferred_element_type=F32)
            s = jnp.where(vis, s, NEG)
            m = jnp.max(s, axis=0, keepdims=True)
            e = jnp.where(vis, jnp.exp2(s - m), 0.0)
            p = e / jnp.maximum(jnp.sum(e, axis=0, keepdims=True), 1e-30)
            pc = p.astype(vct_ref.dtype)
            o = jnp.dot(vct_ref[g], pc, preferred_element_type=F32)
            oc_ref[h] = o * g_ref[h:h + 1, :]
            imp = imp + jnp.dot(mt_ref[...], pc, preferred_element_type=F32)
        imp = jnp.where(forced, jnp.inf, imp)
        imp = jnp.where(causal_b, imp, -jnp.inf)
        bias = jnp.full((NSB, TQ), NEG, F32)
        for _ in range(n_sel):
            m = jnp.max(imp, axis=0, keepdims=True)
            idx = jnp.min(jnp.where(imp == m, blk_f, float(NSB)), axis=0, keepdims=True)
            hit = blk_f == idx
            bias = jnp.where(hit, jnp.where(m > -jnp.inf, 0.0, bias), bias)
            imp = jnp.where(hit, -jnp.inf, imp)
        sb_ref[g] = bias


def _nsa_cmp(q_t, kc, vc_t, m_t, gates_t, S, n_sel):
    TQ = min(TQ_ATT, S)
    NC = kc.shape[1]
    NSB = m_t.shape[0]
    return pl.pallas_call(
        functools.partial(_nsa_cmp_body, TQ=TQ, NC=NC, NSB=NSB, n_sel=n_sel),
        out_shape=[jax.ShapeDtypeStruct((H_B, HEAD_DIM, S), F32),
                   jax.ShapeDtypeStruct((G_B, NSB, S), F32)],
        grid=(S // TQ,),
        in_specs=[pl.BlockSpec((H_B, HEAD_DIM, TQ), lambda i: (0, 0, i)),
                  pl.BlockSpec((G_B, NC, HEAD_DIM), lambda i: (0, 0, 0)),
                  pl.BlockSpec((G_B, HEAD_DIM, NC), lambda i: (0, 0, 0)),
                  pl.BlockSpec((NSB, NC), lambda i: (0, 0)),
                  pl.BlockSpec((N_GATE, TQ), lambda i: (0, i))],
        out_specs=[pl.BlockSpec((H_B, HEAD_DIM, TQ), lambda i: (0, 0, i)),
                   pl.BlockSpec((G_B, NSB, TQ), lambda i: (0, 0, i))],
        compiler_params=_cp("parallel"),
        name="nsa_cmp_select",
    )(q_t, kc, vc_t, m_t, gates_t)


def _nsa_win_body(q_ref, *rest, TQ, nprev):
    k_refs = rest[:nprev + 1]
    v_refs = rest[nprev + 1:2 * nprev + 2]
    g_ref, o_ref = rest[2 * nprev + 2], rest[2 * nprev + 3]
    qb = pl.program_id(0)
    nk = (nprev + 1) * TQ
    kpos = (qb - nprev) * TQ + lax.broadcasted_iota(jnp.int32, (nk, TQ), 0)
    tq = qb * TQ + lax.broadcasted_iota(jnp.int32, (nk, TQ), 1)
    ok = (kpos >= 0) & (kpos <= tq) & (kpos > tq - WINDOW)
    for g in range(G_B):
        ks = [k_refs[b][:, g * HEAD_DIM:(g + 1) * HEAD_DIM] for b in range(nprev + 1)]
        for hh in range(HPG_B):
            h = g * HPG_B + hh
            q = q_ref[h]
            s = jnp.concatenate([jnp.dot(k, q, preferred_element_type=F32) for k in ks], axis=0)
            s = jnp.where(ok, s, NEG)
            m = jnp.max(s, axis=0, keepdims=True)
            e = jnp.where(ok, jnp.exp2(s - m), 0.0)
            p = (e / jnp.maximum(jnp.sum(e, axis=0, keepdims=True), 1e-30)).astype(v_refs[0].dtype)
            o = jnp.zeros((HEAD_DIM, TQ), F32)
            for b in range(nprev + 1):
                o = o + jnp.dot(v_refs[b][g], p[b * TQ:(b + 1) * TQ, :], preferred_element_type=F32)
            o_ref[h] = o * g_ref[2 * H_B + h:2 * H_B + h + 1, :]


def _nsa_win(q_t, kvb, vw_t, gates_t, S):
    TQ = min(TQ_ATT, S)
    nprev = WINDOW // TQ
    cw = G_B * HEAD_DIM
    kcol = (2 * 2 * G_B * HEAD_DIM) // cw
    k_specs = [pl.BlockSpec((TQ, cw), lambda i, d=d: (jnp.maximum(i - d, 0), kcol)) for d in range(nprev, -1, -1)]
    v_specs = [pl.BlockSpec((G_B, HEAD_DIM, TQ), lambda i, d=d: (0, 0, jnp.maximum(i - d, 0))) for d in range(nprev, -1, -1)]
    return pl.pallas_call(
        functools.partial(_nsa_win_body, TQ=TQ, nprev=nprev),
        out_shape=jax.ShapeDtypeStruct((H_B, HEAD_DIM, S), F32),
        grid=(S // TQ,),
        in_specs=[pl.BlockSpec((H_B, HEAD_DIM, TQ), lambda i: (0, 0, i))] + k_specs + v_specs
                 + [pl.BlockSpec((N_GATE, TQ), lambda i: (0, i))],
        out_specs=pl.BlockSpec((H_B, HEAD_DIM, TQ), lambda i: (0, 0, i)),
        compiler_params=_cp("parallel"),
        name="nsa_window",
    )(q_t, *([kvb] * (nprev + 1)), *([vw_t] * (nprev + 1)), gates_t)


def _nsa_sel_body(q_ref, k_ref, v_ref, sb_ref, g_ref, oc_ref, ow_ref, o_ref, m_ref, acc_ref,
                  s_ref, p_ref, bias_ref, *, TQ, TK, nkt):
    qb, kt = pl.program_id(0), pl.program_id(1)
    last = ((qb + 1) * TQ - 1) // TK
    bpt = TK // SLC_BLOCK

    @pl.when(kt == 0)
    def _():
        _flash_init(m_ref, acc_ref)

    @pl.when(kt <= last)
    def _():
        kpos = kt * TK + lax.broadcasted_iota(jnp.int32, (TK, TQ), 0)
        tq = qb * TQ + lax.broadcasted_iota(jnp.int32, (TK, TQ), 1)
        causal = kpos <= tq
        b0 = pl.multiple_of(kt * bpt, bpt)
        for g in range(G_B):
            bb = sb_ref[g, pl.ds(b0, bpt), :]
            btok = jnp.concatenate(
                [jnp.broadcast_to(bb[r:r + 1, :], (SLC_BLOCK, TQ)) for r in range(bpt)], axis=0)
            bias_ref[g] = jnp.where(causal, btok, NEG)

        def score(h):
            g = h // HPG_B
            k = k_ref[:, g * HEAD_DIM:(g + 1) * HEAD_DIM]
            return jnp.dot(k, q_ref[h], preferred_element_type=F32) + bias_ref[g]

        _flash_heads(H_B, score, lambda h: v_ref[h // HPG_B], m_ref, acc_ref, s_ref, p_ref)

    @pl.when(kt == nkt - 1)
    def _():
        for h in range(H_B):
            o = _flash_out(acc_ref, h) * g_ref[H_B + h:H_B + h + 1, :] + oc_ref[h] + ow_ref[h]
            o_ref[h] = o.astype(o_ref.dtype)


def _nsa_sel(q_t, kvb, vs_t, sel_bias, gates_t, oc_t, ow_t, S):
    TQ, TK = min(TQ_ATT, S), min(TK_ATT, S)
    nqb, nkt = S // TQ, S // TK
    NSB = sel_bias.shape[1]
    cw = G_B * HEAD_DIM
    kcol = (1 * 2 * G_B * HEAD_DIM) // cw

    def kmap(qb, kt):
        return jnp.minimum(kt, ((qb + 1) * TQ - 1) // TK)

    hblk = pl.BlockSpec((H_B, HEAD_DIM, TQ), lambda qb, kt: (0, 0, qb))
    return pl.pallas_call(
        functools.partial(_nsa_sel_body, TQ=TQ, TK=TK, nkt=nkt),
        out_shape=jax.ShapeDtypeStruct((H_B, HEAD_DIM, S), CDT),
        grid=(nqb, nkt),
        in_specs=[hblk,
                  pl.BlockSpec((TK, cw), lambda qb, kt: (kmap(qb, kt), kcol)),
                  pl.BlockSpec((G_B, DV_EXT, TK), lambda qb, kt: (0, 0, kmap(qb, kt))),
                  pl.BlockSpec((G_B, NSB, TQ), lambda qb, kt: (0, 0, qb)),
                  pl.BlockSpec((N_GATE, TQ), lambda qb, kt: (0, qb)),
                  hblk, hblk],
        out_specs=hblk,
        scratch_shapes=_flash_scratch(H_B, TQ, TK) + [pltpu.VMEM((G_B, TK, TQ), F32)],
        compiler_params=_cp("parallel", "arbitrary"),
        name="nsa_select_attn",
    )(q_t, kvb, vs_t, sel_bias, gates_t, oc_t, ow_t)


def _rot_cols(w, half, pad):
    parts = [-w[..., half:2 * half], w[..., :half]]
    if pad:
        parts.append(jnp.zeros(w.shape[:-1] + (pad,), w.dtype))
    return jnp.concatenate(parts, axis=-1)


def _prep_mixer_weights(w_in, w_uq, w_iq, w_uk, w_uv, gate_b):
    D = w_in.shape[0]
    cuts = np.cumsum(IN_SPLITS)[:-1].tolist()
    c_q, c_kv, k_r, k_idx, w_idx, q_b, kv_b, gate, q_m = jnp.split(w_in, cuts, axis=1)
    z = lambda n: jnp.zeros((D, n), w_in.dtype)
    w_pa = jnp.concatenate([
        c_q, c_kv, k_r, k_idx,
        _rot_cols(k_r, D_ROPE // 2, 0), _rot_cols(k_idx, D_IDX_ROPE // 2, D_IDX - D_IDX_ROPE),
        w_idx, z(LANE - H_IDX), gate, z(LANE - N_GATE)], axis=1).astype(CDT)
    w_pq = jnp.concatenate([q_b, q_m], axis=1).astype(CDT)
    w_pkv = kv_b.astype(CDT)
    gb = jnp.concatenate([gate_b, jnp.zeros((LANE - N_GATE,), F32)]).reshape(1, LANE)

    w3 = w_uq.reshape(Q_LORA, H_A, D_NOPE + D_ROPE)
    nope, pe = w3[..., :D_NOPE], w3[..., D_NOPE:]
    zq = jnp.zeros((Q_LORA, H_A, LANE - D_ROPE), w_uq.dtype)
    main = jnp.concatenate([nope, pe, zq], axis=-1).reshape(Q_LORA, H_A * 2 * LANE)
    rot = _rot_cols(pe, D_ROPE // 2, LANE - D_ROPE).reshape(Q_LORA, H_A * LANE)
    w_q = jnp.concatenate([main, rot], axis=1).astype(CDT)

    wi3 = w_iq.reshape(Q_LORA, H_IDX, D_IDX)
    roti = _rot_cols(wi3, D_IDX_ROPE // 2, D_IDX - D_IDX_ROPE).reshape(Q_LORA, H_IDX * D_IDX)
    w_qi = jnp.concatenate([w_iq, roti], axis=1).astype(CDT)

    w_kv = jnp.concatenate([w_uk.reshape(KV_LORA, W_A), w_uv.reshape(KV_LORA, W_A)], axis=1).astype(CDT)
    return w_pa, w_pq, w_pkv, gb, w_q, w_qi, w_kv


def _cmp_to_slc_t(S):
    n_c = S // CMP_STRIDE
    n_sb = S // SLC_BLOCK
    cs = np.arange(n_c) * CMP_STRIDE
    ss = np.arange(n_sb) * SLC_BLOCK
    m = (cs[None, :] < ss[:, None] + SLC_BLOCK) & (cs[None, :] + CMP_LEN > ss[:, None])
    m[:, (S - CMP_LEN) // CMP_STRIDE + 1:] = False
    return jnp.asarray(m.astype(np.float32)).astype(CDT)


def _mixer(x, xc, mem_c, tabs, lw, S):
    (w_pa, w_pq, w_pkv, gb, w_q, w_qi, w_kv, g_cq, g_ckv, cmp_w1, cmp_b1, cmp_w2, cmp_b2, cmp_pe,
     w_mem_kv, w_out) = lw
    rowtab = lambda t: (tabs[t], "row")

    cq, ckv, kk, wi, gates = _mm_rows(
        xc, w_pa, _epi_dsa_prep,
        [(g_cq.reshape(1, -1), "full"), (g_ckv.reshape(1, -1), "full"), (gb, "full"), rowtab(0), rowtab(1)],
        [(Q_LORA, CDT), (KV_LORA, CDT), (LANE, CDT), (LANE, F32), (LANE, F32)], "proj_dsa")
    q_b, q_m = _mm_rows(xc, w_pq, _epi_q_nsa_mem, [rowtab(6), rowtab(7)],
                        [(W_B, CDT), (W_M, CDT)], "proj_q")
    kvb, = _mm_rows(xc, w_pkv, _epi_kv_nsa, [rowtab(6), rowtab(7)], [(N_KVB, CDT)], "proj_kv")

    q_ext, = _mm_rows(cq, w_q, _epi_dsa_q, [rowtab(2), rowtab(3)], [(H_A * 2 * LANE, CDT)], "dsa_q")
    q_idx, = _mm_rows(cq, w_qi, _epi_idx_q, [rowtab(4), rowtab(5)], [(H_IDX * D_IDX, CDT)], "dsa_qi")
    kv_a = _mm(ckv, w_kv, CDT, "dsa_kv")
    k_pe, k_i = kk[:, :D_ROPE], kk[:, D_ROPE:]
    k_nope = kv_a[:, :W_A].reshape(S, H_A, D_NOPE).transpose(1, 0, 2)
    k_ext = jnp.concatenate([
        k_nope, jnp.broadcast_to(k_pe[None], (H_A, S, D_ROPE)),
        jnp.zeros((H_A, S, LANE - D_ROPE), CDT)], axis=-1)
    va_t = kv_a[:, W_A:].reshape(S, H_A, HEAD_DIM).transpose(1, 2, 0)
    qa_t = q_ext.reshape(S, H_A, 2 * LANE).transpose(1, 2, 0)
    nqi = S // TQ_IDX
    qi_t = q_idx.reshape(nqi, TQ_IDX, H_IDX, D_IDX).transpose(0, 3, 2, 1).reshape(nqi, D_IDX, H_IDX * TQ_IDX)
    w_t = wi[:, :H_IDX].reshape(nqi, TQ_IDX, H_IDX).transpose(0, 2, 1)
    topk = min(TOPK_MAX, S // 4)
    dsa_bias = _indexer(qi_t, w_t, k_i, S, topk)
    oa_t = _dsa_attn(qa_t, k_ext, _with_ones_row(va_t), dsa_bias, S)
    o_a = oa_t.transpose(2, 0, 1).reshape(S, W_A)

    gates_t = gates[:, :N_GATE].T
    qb_t = q_b.reshape(S, H_B, HEAD_DIM).transpose(1, 2, 0)
    cw = G_B * HEAD_DIM
    nch = S // CMP_STRIDE
    ld = CMP_LEN * HEAD_DIM

    def cmp_blocks(t):
        ch = t.reshape(nch, CMP_STRIDE, G_B, HEAD_DIM).transpose(2, 0, 1, 3).reshape(G_B, nch, CMP_STRIDE * HEAD_DIM)
        nxt = jnp.concatenate([ch[:, 1:], jnp.zeros((G_B, 1, CMP_STRIDE * HEAD_DIM), ch.dtype)], axis=1)
        return jnp.concatenate([ch, nxt], axis=-1)

    blocks = jnp.stack([cmp_blocks(kvb[:, 0:cw]), cmp_blocks(kvb[:, cw:2 * cw])])
    cmp_out = _compress(blocks, cmp_pe.reshape(2, 1, ld), cmp_w1.astype(CDT), cmp_b1.reshape(2, 1, CMP_HID),
                        cmp_w2.astype(CDT), cmp_b2.reshape(2, 1, HEAD_DIM))
    kc = cmp_out[0]
    vc_t = cmp_out[1].transpose(0, 2, 1)
    n_sb = S // SLC_BLOCK
    oc_t, sel_bias = _nsa_cmp(qb_t, kc, vc_t, _cmp_to_slc_t(S), gates_t, S, min(N_SLC, n_sb))
    vs_t = kvb[:, 3 * cw:4 * cw].reshape(S, G_B, HEAD_DIM).transpose(1, 2, 0)
    vw_t = kvb[:, 5 * cw:6 * cw].reshape(S, G_B, HEAD_DIM).transpose(1, 2, 0)
    ow_t = _nsa_win(qb_t, kvb, vw_t, gates_t, S)
    ob_t = _nsa_sel(qb_t, kvb, _with_ones_row(vs_t), sel_bias, gates_t, oc_t, ow_t, S)
    o_b = ob_t.transpose(2, 0, 1).reshape(S, W_B)

    n_mem = mem_c.shape[0]
    mem_kv = _mm(mem_c, w_mem_kv.astype(CDT), CDT, "mem_kv")
    mk_t = mem_kv[:, :W_M].reshape(n_mem, H_M, D_M).transpose(1, 2, 0)
    mv = mem_kv[:, W_M:].reshape(n_mem, H_M, D_M).transpose(1, 0, 2)
    o_m = _mem_attn(q_m, mk_t, mv)

    mix = _mm(jnp.concatenate([o_a, o_b, o_m], axis=1), w_out.astype(CDT), F32, "mix_out")
    return mix


def kernel(x, mem, positions, w_in, g_cq, g_ckv, w_uq, w_iq, w_uk, w_uv, cmp_w1, cmp_b1, cmp_w2, cmp_b2,
           cmp_pe, gate_b, w_mem_kv, w_out, ln1_g, ln1_b, ln2_g, ln2_b, ffn_wg, ffn_wu, ffn_wd,
           router_w, moe_wg, moe_wu, moe_wd):
    B, S, D = x.shape
    assert B == 1 and S % max(TK_ATT, TK_IDX) == 0 and WINDOW % TQ_ATT == 0
    depth = w_in.shape[0]
    alpha = (2 * depth) ** 0.25
    xs = x[0]
    mem_c = mem[0].astype(CDT)
    tabs = _rope_tables(positions)
    for i in range(depth):
        lw = _prep_mixer_weights(w_in[i], w_uq[i], w_iq[i], w_uk[i], w_uv[i], gate_b[i]) + (
            g_cq[i], g_ckv[i], cmp_w1[i], cmp_b1[i], cmp_w2[i], cmp_b2[i], cmp_pe[i], w_mem_kv[i], w_out[i])
        mix = _mixer(xs, xs.astype(CDT), mem_c, tabs, lw, S)
        xs = _res_ln(xs, mix, ln1_g[i], ln1_b[i], alpha)
        xc = xs.astype(CDT)
        j = i // 2
        if i % 2 == 0:
            h = _swiglu_up(xc, ffn_wg[j].astype(CDT), ffn_wu[j].astype(CDT), "ffn_up")
            f = _mm(h, ffn_wd[j].astype(CDT), F32, "ffn_down")
        else:
            n_exp, _, dfe = moe_wg[j].shape[0], moe_wg[j].shape[1], moe_wg[j].shape[2]
            rw = jnp.concatenate([router_w[j], jnp.zeros((D, LANE - n_exp), F32)], axis=1).astype(CDT)
            gate, = _mm_rows(xc, rw, _make_epi_router(n_exp), [], [(LANE, F32)], "moe_router")
            scale = gate[:, :n_exp].T.reshape(n_exp, S, 1)
            wg = moe_wg[j].transpose(1, 0, 2).reshape(D, n_exp * dfe).astype(CDT)
            wu = moe_wu[j].transpose(1, 0, 2).reshape(D, n_exp * dfe).astype(CDT)
            h = _swiglu_up(xc, wg, wu, "moe_up", scale=scale, n_per_scale=dfe)
            f = _mm(h, moe_wd[j].reshape(n_exp * dfe, D).astype(CDT), F32, "moe_down")
        xs = _res_ln(xs, f, ln2_g[i], ln2_b[i], alpha)
    return xs[None]
```

```python
import functools
import math

import numpy as np
import jax
import jax.numpy as jnp
from jax import lax
from jax.experimental import pallas as pl
from jax.experimental.pallas import tpu as pltpu

F32 = jnp.float32
CDT = jnp.bfloat16

HEAD_DIM = 128
H_A = 12
H_B = 12
H_M = 4
D_M = 256
Q_LORA = 1024
KV_LORA = 512
D_NOPE = 128
D_ROPE = 64
H_IDX = 32
D_IDX = 64
D_IDX_ROPE = 32
TOPK_MAX = 256
G_B = 4
HPG_B = H_B // G_B
CMP_LEN = 32
CMP_STRIDE = 16
CMP_HID = 256
SLC_BLOCK = 64
N_SLC = 16
WINDOW = 512
TOP_E = 2
ROPE_THETA = 10000.0
EPS = 1e-5

W_A = H_A * HEAD_DIM
W_B = H_B * HEAD_DIM
W_M = H_M * D_M
N_KVB = 3 * 2 * G_B * HEAD_DIM
N_GATE = 3 * H_B
IN_SPLITS = (Q_LORA, KV_LORA, D_ROPE, D_IDX, H_IDX, W_B, N_KVB, N_GATE, W_M)

LANE = 128
NEG = -1e30
LOG2E = math.log2(math.e)
INT_MIN = -2147483648
VMEM_LIMIT = 48 * 1024 * 1024

TQ_IDX = 128
TK_IDX = 512
TQ_ATT = 256
TK_ATT = 1024


def _cp(*sem):
    return pltpu.CompilerParams(dimension_semantics=sem, vmem_limit_bytes=VMEM_LIMIT)


def _pick(n, cands):
    for c in cands:
        if c <= n and n % c == 0:
            return c
    return n


def _mm_body(x_ref, w_ref, o_ref, *scratch, nk):
    part = jnp.dot(x_ref[...], w_ref[...], preferred_element_type=F32)
    if nk == 1:
        o_ref[...] = part.astype(o_ref.dtype)
        return
    acc_ref, = scratch
    k = pl.program_id(2)

    @pl.when(k == 0)
    def _():
        acc_ref[...] = part

    @pl.when(k > 0)
    def _():
        acc_ref[...] += part

    @pl.when(k == nk - 1)
    def _():
        o_ref[...] = acc_ref[...].astype(o_ref.dtype)


def _mm(x, w, out_dtype, name):
    M, K = x.shape
    N = w.shape[1]
    tm = _pick(M, (512, 256, 128))
    tn = _pick(N, (1024, 512, 256, 128))
    tk = K
    for parts in (1, 2, 4, 8, 16):
        if K % parts == 0 and (K // parts) % LANE == 0 and (K // parts) * max(tm, tn) * 2 <= 12 * 2 ** 20:
            tk = K // parts
            break
    nk = K // tk
    scratch = [pltpu.VMEM((tm, tn), F32)] if nk > 1 else []
    return pl.pallas_call(
        functools.partial(_mm_body, nk=nk),
        out_shape=jax.ShapeDtypeStruct((M, N), out_dtype),
        grid=(M // tm, N // tn, nk),
        in_specs=[pl.BlockSpec((tm, tk), lambda i, j, k: (i, k)),
                  pl.BlockSpec((tk, tn), lambda i, j, k: (k, j))],
        out_specs=pl.BlockSpec((tm, tn), lambda i, j, k: (i, j)),
        scratch_shapes=scratch,
        compiler_params=_cp("parallel", "parallel", "arbitrary"),
        name=name,
    )(x, w)


def _mm_rows_body(*refs, nk, n_aux, n_out, epi):
    x_ref, w_ref = refs[0], refs[1]
    aux = refs[2:2 + n_aux]
    outs = refs[2 + n_aux:2 + n_aux + n_out]
    acc_ref = refs[2 + n_aux + n_out]
    k = pl.program_id(1)
    N = w_ref.shape[1]
    cw = _pick(N, (512, 256, 128))
    x = x_ref[...]

    @pl.when(k == 0)
    def _():
        for c in range(0, N, cw):
            acc_ref[:, c:c + cw] = jnp.dot(x, w_ref[:, c:c + cw], preferred_element_type=F32)

    @pl.when(k > 0)
    def _():
        for c in range(0, N, cw):
            acc_ref[:, c:c + cw] += jnp.dot(x, w_ref[:, c:c + cw], preferred_element_type=F32)

    @pl.when(k == nk - 1)
    def _():
        epi(acc_ref, aux, outs)


def _mm_rows(x, w, epi, aux, outs, name, tm=None):
    M, K = x.shape
    N = w.shape[1]
    tm = tm or _pick(M, (512, 256, 128))
    tk = _pick(K, (1024, 512, 256, 128))
    nk = K // tk
    in_specs = [pl.BlockSpec((tm, tk), lambda i, k: (i, k)),
                pl.BlockSpec((tk, N), lambda i, k: (k, 0))]
    for a, kind in aux:
        if kind == "row":
            in_specs.append(pl.BlockSpec((tm, a.shape[1]), lambda i, k: (i, 0)))
        else:
            in_specs.append(pl.BlockSpec(a.shape, lambda i, k, nd=a.ndim: (0,) * nd))
    res = pl.pallas_call(
        functools.partial(_mm_rows_body, nk=nk, n_aux=len(aux), n_out=len(outs), epi=epi),
        out_shape=[jax.ShapeDtypeStruct((M, c), dt) for c, dt in outs],
        grid=(M // tm, nk),
        in_specs=in_specs,
        out_specs=[pl.BlockSpec((tm, c), lambda i, k: (i, 0)) for c, _ in outs],
        scratch_shapes=[pltpu.VMEM((tm, N), F32)],
        compiler_params=_cp("parallel", "arbitrary"),
        name=name,
    )(x, w, *[a for a, _ in aux])
    return res


def _epi_plain(acc_ref, aux, outs):
    outs[0][...] = acc_ref[...].astype(outs[0].dtype)


def _rope_tab_body(pos_ref, inv_ref, sgn_ref, o_ref):
    pos = pos_ref[...].astype(F32)
    nt = inv_ref.shape[0]
    for t in range(nt):
        ang = pos * inv_ref[t:t + 1, :]
        o_ref[2 * t] = jnp.cos(ang)
        o_ref[2 * t + 1] = jnp.sin(ang) * sgn_ref[t:t + 1, :]


def _inv_freq(d):
    return jnp.power(ROPE_THETA, -jnp.arange(0, d, 2, dtype=F32) / d)


def _rope_tables(positions):
    S = positions.shape[1]
    i32, i16, i64 = _inv_freq(D_ROPE), _inv_freq(D_IDX_ROPE), _inv_freq(HEAD_DIM)
    z32, z64 = jnp.zeros((32,), F32), jnp.zeros((64,), F32)
    inv = jnp.stack([
        jnp.concatenate([i32, i32, i16, i16, z32]),
        jnp.concatenate([i32, i32, z64]),
        jnp.concatenate([i16, i16, z32, i16, i16, z32]),
        jnp.concatenate([i64, i64]),
    ])
    sgn = jnp.ones((4, LANE), F32).at[3, :64].set(-1.0)
    tm = _pick(S, (512, 256, 128))
    return pl.pallas_call(
        _rope_tab_body,
        out_shape=jax.ShapeDtypeStruct((8, S, LANE), F32),
        grid=(S // tm,),
        in_specs=[pl.BlockSpec((tm, 1), lambda i: (i, 0)),
                  pl.BlockSpec((4, LANE), lambda i: (0, 0)),
                  pl.BlockSpec((4, LANE), lambda i: (0, 0))],
        out_specs=pl.BlockSpec((8, tm, LANE), lambda i: (0, i, 0)),
        compiler_params=_cp("parallel"),
        name="rope_tables",
    )(positions.reshape(S, 1), inv, sgn)


N_PA = Q_LORA + KV_LORA + 4 * LANE


def _rms(v, g):
    return v * lax.rsqrt(jnp.mean(v * v, axis=-1, keepdims=True) + EPS) * g


def _epi_dsa_prep(acc_ref, aux, outs):
    gq_ref, gkv_ref, gb_ref, tc_ref, ts_ref = aux
    cq_ref, ckv_ref, kk_ref, wi_ref, gate_ref = outs
    o = Q_LORA + KV_LORA
    cq_ref[...] = _rms(acc_ref[:, 0:Q_LORA], gq_ref[...]).astype(cq_ref.dtype)
    ckv_ref[...] = _rms(acc_ref[:, Q_LORA:o], gkv_ref[...]).astype(ckv_ref.dtype)
    kk = acc_ref[:, o:o + LANE] * tc_ref[...] + acc_ref[:, o + LANE:o + 2 * LANE] * ts_ref[...]
    kk_ref[...] = kk.astype(kk_ref.dtype)
    wi_ref[...] = acc_ref[:, o + 2 * LANE:o + 3 * LANE] * (H_IDX ** -0.5 * D_IDX ** -0.5)
    gate_ref[...] = jax.nn.sigmoid(acc_ref[:, o + 3 * LANE:o + 4 * LANE] + gb_ref[...])


def _epi_q_nsa_mem(acc_ref, aux, outs):
    tc_ref, ts_ref = aux
    qb_ref, qm_ref = outs
    tc, ts = tc_ref[...], ts_ref[...]
    for h in range(H_B):
        t = acc_ref[:, h * HEAD_DIM:(h + 1) * HEAD_DIM]
        r = (t * tc + pltpu.roll(t, HEAD_DIM // 2, 1) * ts) * (HEAD_DIM ** -0.5 * LOG2E)
        qb_ref[:, h * HEAD_DIM:(h + 1) * HEAD_DIM] = r.astype(qb_ref.dtype)
    qm_ref[...] = (acc_ref[:, W_B:W_B + W_M] * (D_M ** -0.5 * LOG2E)).astype(qm_ref.dtype)


def _epi_kv_nsa(acc_ref, aux, outs):
    tc_ref, ts_ref = aux
    o_ref, = outs
    tc, ts = tc_ref[...], ts_ref[...]
    for br in range(3):
        for kv in range(2):
            for g in range(G_B):
                c = ((br * 2 + kv) * G_B + g) * HEAD_DIM
                t = acc_ref[:, c:c + HEAD_DIM]
                if kv == 0:
                    t = t * tc + pltpu.roll(t, HEAD_DIM // 2, 1) * ts
                o_ref[:, c:c + HEAD_DIM] = t.astype(o_ref.dtype)


def _epi_dsa_q(acc_ref, aux, outs):
    tc_ref, ts_ref = aux
    o_ref, = outs
    tc, ts = tc_ref[...], ts_ref[...]
    sc = (D_NOPE + D_ROPE) ** -0.5 * LOG2E
    W = 2 * LANE
    for h in range(H_A):
        o_ref[:, h * W:h * W + LANE] = (acc_ref[:, h * W:h * W + LANE] * sc).astype(o_ref.dtype)
        pe = acc_ref[:, h * W + LANE:(h + 1) * W] * tc + acc_ref[:, H_A * W + h * LANE:H_A * W + (h + 1) * LANE] * ts
        o_ref[:, h * W + LANE:(h + 1) * W] = (pe * sc).astype(o_ref.dtype)


def _epi_idx_q(acc_ref, aux, outs):
    tc_ref, ts_ref = aux
    o_ref, = outs
    tc, ts = tc_ref[...], ts_ref[...]
    n = H_IDX * D_IDX
    for c in range(n // LANE):
        r = acc_ref[:, c * LANE:(c + 1) * LANE] * tc + acc_ref[:, n + c * LANE:n + (c + 1) * LANE] * ts
        o_ref[:, c * LANE:(c + 1) * LANE] = r.astype(o_ref.dtype)


def _res_ln_body(x_ref, f_ref, g_ref, b_ref, o_ref, *, alpha):
    v = alpha * x_ref[...] + f_ref[...].astype(F32)
    mu = jnp.mean(v, axis=-1, keepdims=True)
    d = v - mu
    var = jnp.mean(d * d, axis=-1, keepdims=True)
    o_ref[...] = d * lax.rsqrt(var + EPS) * g_ref[...] + b_ref[...]


def _res_ln(x, f, g, b, alpha):
    S, D = x.shape
    tm = _pick(S, (256, 128))
    row = pl.BlockSpec((tm, D), lambda i: (i, 0))
    vec = pl.BlockSpec((1, D), lambda i: (0, 0))
    return pl.pallas_call(
        functools.partial(_res_ln_body, alpha=alpha),
        out_shape=jax.ShapeDtypeStruct((S, D), F32),
        grid=(S // tm,),
        in_specs=[row, row, vec, vec],
        out_specs=row,
        compiler_params=_cp("parallel"),
        name="res_ln",
    )(x, f, g.reshape(1, D), b.reshape(1, D))


def _swiglu_up_body(x_ref, wg_ref, wu_ref, *rest, scaled):
    if scaled:
        s_ref, o_ref = rest
    else:
        o_ref, = rest
    x = x_ref[...]
    g = jnp.dot(x, wg_ref[...], preferred_element_type=F32)
    u = jnp.dot(x, wu_ref[...], preferred_element_type=F32)
    h = g * jax.nn.sigmoid(g) * u
    if scaled:
        h = h * s_ref[...]
    o_ref[...] = h.astype(o_ref.dtype)


def _swiglu_up(x, wg, wu, name, scale=None, n_per_scale=None):
    M, K = x.shape
    N = wg.shape[1]
    tm = _pick(M, (512, 256, 128))
    if scale is None:
        tn = _pick(N, (512, 256, 128))
    else:
        tn = _pick(n_per_scale, (512, 256, 128))
    in_specs = [pl.BlockSpec((tm, K), lambda i, j: (i, 0)),
                pl.BlockSpec((K, tn), lambda i, j: (0, j)),
                pl.BlockSpec((K, tn), lambda i, j: (0, j))]
    args = [x, wg, wu]
    if scale is not None:
        per = n_per_scale // tn
        in_specs.append(pl.BlockSpec((None, tm, 1), lambda i, j: (j // per, i, 0)))
        args.append(scale)
    return pl.pallas_call(
        functools.partial(_swiglu_up_body, scaled=scale is not None),
        out_shape=jax.ShapeDtypeStruct((M, N), CDT),
        grid=(M // tm, N // tn),
        in_specs=in_specs,
        out_specs=pl.BlockSpec((tm, tn), lambda i, j: (i, j)),
        compiler_params=_cp("parallel", "parallel"),
        name=name,
    )(*args)


def _make_epi_router(n_exp):
    def epi(acc_ref, aux, outs):
        o_ref, = outs
        lg = acc_ref[...]
        lane = lax.broadcasted_iota(jnp.int32, lg.shape, 1)
        lg = jnp.where(lane < n_exp, lg, -jnp.inf)
        m1 = jnp.max(lg, axis=-1, keepdims=True)
        i1 = jnp.min(jnp.where(lg == m1, lane, LANE), axis=-1, keepdims=True)
        l2 = jnp.where(lane == i1, -jnp.inf, lg)
        m2 = jnp.max(l2, axis=-1, keepdims=True)
        i2 = jnp.min(jnp.where(l2 == m2, lane, LANE), axis=-1, keepdims=True)
        e2 = jnp.exp(m2 - m1)
        den = 1.0 + e2
        o_ref[...] = jnp.where(lane == i1, 1.0 / den, 0.0) + jnp.where(lane == i2, e2 / den, 0.0)
    return epi


def _mem_attn_body(q_ref, kt_ref, v_ref, o_ref):
    for h in range(H_M):
        q = q_ref[:, h * D_M:(h + 1) * D_M]
        s = jnp.dot(q, kt_ref[h], preferred_element_type=F32)
        m = jnp.max(s, axis=-1, keepdims=True)
        e = jnp.exp2(s - m)
        p = e / jnp.sum(e, axis=-1, keepdims=True)
        o = jnp.dot(p.astype(v_ref.dtype), v_ref[h], preferred_element_type=F32)
        o_ref[:, h * D_M:(h + 1) * D_M] = o.astype(o_ref.dtype)


def _mem_attn(q_m, k_t, v):
    S = q_m.shape[0]
    n_mem = v.shape[1]
    tm = _pick(S, (512, 256, 128))
    return pl.pallas_call(
        _mem_attn_body,
        out_shape=jax.ShapeDtypeStruct((S, W_M), CDT),
        grid=(S // tm,),
        in_specs=[pl.BlockSpec((tm, W_M), lambda i: (i, 0)),
                  pl.BlockSpec((H_M, D_M, n_mem), lambda i: (0, 0, 0)),
                  pl.BlockSpec((H_M, n_mem, D_M), lambda i: (0, 0, 0))],
        out_specs=pl.BlockSpec((tm, W_M), lambda i: (i, 0)),
        compiler_params=_cp("parallel"),
        name="mem_attn",
    )(q_m, k_t, v)


def _float_key(x):
    b = pltpu.bitcast(x, jnp.int32)
    return jnp.where(b >= 0, b, b ^ jnp.int32(0x7FFFFFFF))


def _indexer_body(qi_ref, w_ref, ki_ref, o_ref, keys_ref, *, S, topk, TQ, TK, HC):
    j = pl.program_id(0)
    n_valid = (j + 1) * TQ
    nkt = (n_valid + TK - 1) // TK

    @pl.when(j == 0)
    def _():
        keys_ref[...] = jnp.full(keys_ref.shape, INT_MIN, jnp.int32)

    qpos = j * TQ + lax.broadcasted_iota(jnp.int32, (TK, TQ), 1)
    krow = lax.broadcasted_iota(jnp.int32, (TK, TQ), 0)

    def score_tile(kt, carry):
        r0 = pl.multiple_of(kt * TK, TK)
        k = ki_ref[pl.ds(r0, TK), :]
        acc = jnp.zeros((TK, TQ), F32)
        for hc in range(H_IDX // HC):
            lg = jnp.dot(k, qi_ref[0, :, hc * HC * TQ:(hc + 1) * HC * TQ],
                         preferred_element_type=F32)
            for hh in range(HC):
                h = hc * HC + hh
                acc = acc + jnp.maximum(lg[:, hh * TQ:(hh + 1) * TQ], 0.0) * w_ref[0, h:h + 1, :]
        key = jnp.where(krow + r0 <= qpos, _float_key(acc), INT_MIN)
        keys_ref[pl.ds(r0, TK), :] = key
        return carry

    lax.fori_loop(0, nkt, score_tile, 0)

    RB = 64

    def count_where(pred):
        def body(kt, c):
            r0 = pl.multiple_of(kt * TK, TK)
            hit = jnp.where(pred(keys_ref[pl.ds(r0, TK), :], r0), 1.0, 0.0)
            return c + jnp.sum(hit.reshape(TK // RB, RB, TQ), axis=0)
        c = lax.fori_loop(0, nkt, body, jnp.zeros((RB, TQ), F32))
        return jnp.sum(c, axis=0, keepdims=True)

    def count_ge(cand):
        return count_where(lambda key, r0: key >= cand)

    BPG = 4

    def bit_group(carry):
        g, prefix, cnt_p = carry
        for i in range(BPG):
            bv = jnp.left_shift(jnp.int32(1), 31 - (g * BPG + i))
            cand_u = prefix | bv
            cnt = count_ge(cand_u ^ jnp.int32(INT_MIN))
            ok = cnt >= float(topk)
            prefix = jnp.where(ok, cand_u, prefix)
            cnt_p = jnp.where(ok, cnt, cnt_p)
        return g + 1, prefix, cnt_p

    def undecided(carry):
        g, _, cnt_p = carry
        return (g < 32 // BPG) & (jnp.max(jnp.where(cnt_p == float(topk), 0.0, 1.0)) > 0.0)

    start = (jnp.int32(0), jnp.zeros((1, TQ), jnp.int32), jnp.full((1, TQ), nkt * TK, jnp.int32).astype(F32))
    _, prefix, cnt_ge = lax.while_loop(undecided, bit_group, start)
    thr = prefix ^ jnp.int32(INT_MIN)
    thr_eff = jnp.maximum(thr, jnp.int32(INT_MIN + 1))
    has_tie = jnp.max(jnp.where((cnt_ge > float(topk)) & (thr != jnp.int32(INT_MIN)), 1.0, 0.0)) > 0.0

    def write_rows(select):
        def write_tile(kt, carry):
            r0 = pl.multiple_of(kt * TK, TK)
            sel = select(keys_ref[pl.ds(r0, TK), :], r0)
            o_ref[pl.ds(r0, TK), :] = jnp.where(sel, 0.0, NEG).astype(o_ref.dtype)
            return carry
        lax.fori_loop(0, nkt, write_tile, 0)

    @pl.when(jnp.logical_not(has_tie))
    def _():
        write_rows(lambda key, r0: key >= thr_eff)

    @pl.when(has_tie)
    def _():
        top = thr == jnp.int32(0x7FFFFFFF)
        cnt_gt = jnp.where(top, 0.0, count_ge(jnp.where(top, thr, thr + 1)))
        need = float(topk) - cnt_gt
        lo = jnp.zeros((1, TQ), jnp.int32)
        hi = jnp.full((1, TQ), S - 1, jnp.int32)
        for _ in range(max(1, int(math.ceil(math.log2(S))))):
            mid = (lo + hi) >> 1
            ok = count_where(lambda key, r0: (key == thr) & (krow + r0 <= mid)) >= need
            hi = jnp.where(ok, mid, hi)
            lo = jnp.where(ok, lo, mid + 1)
        write_rows(lambda key, r0: (key > thr_eff) | ((key == thr_eff) & (krow + r0 <= hi)))

    def write_masked(kt, carry):
        o_ref[pl.ds(pl.multiple_of(kt * TK, TK), TK), :] = jnp.full((TK, TQ), NEG, o_ref.dtype)
        return carry

    lax.fori_loop(nkt, S // TK, write_masked, 0)


def _indexer(qi_t, w_t, k_i, S, topk):
    TQ, TK = TQ_IDX, min(TK_IDX, S)
    nqb = S // TQ
    return pl.pallas_call(
        functools.partial(_indexer_body, S=S, topk=topk, TQ=TQ, TK=TK, HC=4),
        out_shape=jax.ShapeDtypeStruct((S, S), jnp.bfloat16),
        grid=(nqb,),
        in_specs=[pl.BlockSpec((1, D_IDX, H_IDX * TQ), lambda j: (j, 0, 0)),
                  pl.BlockSpec((1, H_IDX, TQ), lambda j: (j, 0, 0)),
                  pl.BlockSpec((S, D_IDX), lambda j: (0, 0))],
        out_specs=pl.BlockSpec((S, TQ), lambda j: (0, j)),
        scratch_shapes=[pltpu.VMEM((S, TQ), jnp.int32)],
        compiler_params=_cp("arbitrary"),
        name="dsa_indexer",
    )(qi_t, w_t, k_i)


DV_EXT = HEAD_DIM + 16


def _with_ones_row(v_t):
    H, dv, S = v_t.shape
    return jnp.concatenate([v_t, jnp.ones((H, 1, S), v_t.dtype), jnp.zeros((H, DV_EXT - dv - 1, S), v_t.dtype)],
                           axis=1)


def _flash_heads(n_heads, score_fn, vt_fn, m_ref, acc_ref, s_ref, p_ref):
    s_ref[0] = score_fn(0)
    for h in range(n_heads):
        b = h % 2
        if h + 1 < n_heads:
            s_ref[1 - b] = score_fn(h + 1)
        s = s_ref[b]
        m_old = m_ref[h]
        m_new = jnp.maximum(m_old, jnp.max(s, axis=0, keepdims=True))
        a = jnp.exp2(m_old - m_new)
        p_ref[b] = jnp.exp2((s - m_new).astype(p_ref.dtype))
        acc_ref[h] = a * acc_ref[h] + jnp.dot(vt_fn(h), p_ref[b], preferred_element_type=F32)
        m_ref[h] = m_new


def _flash_init(m_ref, acc_ref):
    m_ref[...] = jnp.full(m_ref.shape, NEG, F32)
    acc_ref[...] = jnp.zeros(acc_ref.shape, F32)


def _flash_out(acc_ref, h):
    return acc_ref[h, :HEAD_DIM, :] / acc_ref[h, HEAD_DIM:HEAD_DIM + 1, :]


def _flash_scratch(n_heads, TQ, TK):
    return [pltpu.VMEM((n_heads, 1, TQ), F32), pltpu.VMEM((n_heads, DV_EXT, TQ), F32),
            pltpu.VMEM((2, TK, TQ), F32), pltpu.VMEM((2, TK, TQ), CDT)]


def _dsa_attn_body(q_ref, k_ref, v_ref, b_ref, o_ref, m_ref, acc_ref, s_ref, p_ref, bias_ref,
                   *, TQ, TK, nkt):
    qb, kt = pl.program_id(0), pl.program_id(1)
    last = ((qb + 1) * TQ - 1) // TK

    @pl.when(kt == 0)
    def _():
        _flash_init(m_ref, acc_ref)

    @pl.when(kt <= last)
    def _():
        bias_ref[...] = b_ref[...].astype(F32)
        _flash_heads(
            H_A,
            lambda h: jnp.dot(k_ref[h], q_ref[h], preferred_element_type=F32) + bias_ref[...],
            lambda h: v_ref[h], m_ref, acc_ref, s_ref, p_ref)

    @pl.when(kt == nkt - 1)
    def _():
        for h in range(H_A):
            o_ref[h] = _flash_out(acc_ref, h).astype(o_ref.dtype)


def _dsa_attn(q_t, k_ext, v_t, bias, S):
    TQ, TK = min(TQ_ATT, S), min(TK_ATT, S)
    nqb, nkt = S // TQ, S // TK

    def kmap(qb, kt):
        return jnp.minimum(kt, ((qb + 1) * TQ - 1) // TK)

    return pl.pallas_call(
        functools.partial(_dsa_attn_body, TQ=TQ, TK=TK, nkt=nkt),
        out_shape=jax.ShapeDtypeStruct((H_A, HEAD_DIM, S), CDT),
        grid=(nqb, nkt),
        in_specs=[pl.BlockSpec((H_A, 2 * LANE, TQ), lambda qb, kt: (0, 0, qb)),
                  pl.BlockSpec((H_A, TK, 2 * LANE), lambda qb, kt: (0, kmap(qb, kt), 0)),
                  pl.BlockSpec((H_A, DV_EXT, TK), lambda qb, kt: (0, 0, kmap(qb, kt))),
                  pl.BlockSpec((TK, TQ), lambda qb, kt: (kmap(qb, kt), qb))],
        out_specs=pl.BlockSpec((H_A, HEAD_DIM, TQ), lambda qb, kt: (0, 0, qb)),
        scratch_shapes=_flash_scratch(H_A, TQ, TK) + [pltpu.VMEM((TK, TQ), F32)],
        compiler_params=_cp("parallel", "arbitrary"),
        name="dsa_attn",
    )(q_t, k_ext, v_t, bias)


def _compress_body(blk_ref, pe_ref, w1_ref, b1_ref, w2_ref, b2_ref, o_ref):
    tb = (blk_ref[0, 0].astype(F32) + pe_ref[0]).astype(w1_ref.dtype)
    hdn = jax.nn.gelu(jnp.dot(tb, w1_ref[0], preferred_element_type=F32) + b1_ref[0])
    out = jnp.dot(hdn.astype(w2_ref.dtype), w2_ref[0], preferred_element_type=F32) + b2_ref[0]
    o_ref[0, 0] = out.astype(o_ref.dtype)


def _compress(blocks, pe, w1, b1, w2, b2):
    _, G, NC, LD = blocks.shape
    hid, D = w2.shape[1], w2.shape[2]
    return pl.pallas_call(
        _compress_body,
        out_shape=jax.ShapeDtypeStruct((2, G, NC, D), CDT),
        grid=(2, G),
        in_specs=[pl.BlockSpec((1, 1, NC, LD), lambda j, g: (j, g, 0, 0)),
                  pl.BlockSpec((1, 1, LD), lambda j, g: (j, 0, 0)),
                  pl.BlockSpec((1, LD, hid), lambda j, g: (j, 0, 0)),
                  pl.BlockSpec((1, 1, hid), lambda j, g: (j, 0, 0)),
                  pl.BlockSpec((1, hid, D), lambda j, g: (j, 0, 0)),
                  pl.BlockSpec((1, 1, D), lambda j, g: (j, 0, 0))],
        out_specs=pl.BlockSpec((1, 1, NC, D), lambda j, g: (j, g, 0, 0)),
        compiler_params=_cp("parallel", "parallel"),
        name="nsa_compress",
    )(blocks, pe, w1, b1, w2, b2)


def _nsa_cmp_body(q_ref, kc_ref, vct_ref, mt_ref, g_ref, oc_ref, sb_ref, *, TQ, NC, NSB, n_sel):
    qb = pl.program_id(0)
    tq = qb * TQ + lax.broadcasted_iota(jnp.int32, (NC, TQ), 1)
    cmp_end = lax.broadcasted_iota(jnp.int32, (NC, TQ), 0) * CMP_STRIDE + (CMP_LEN - 1)
    vis = cmp_end <= tq
    blk = lax.broadcasted_iota(jnp.int32, (NSB, TQ), 0)
    tqs = qb * TQ + lax.broadcasted_iota(jnp.int32, (NSB, TQ), 1)
    cur = jnp.right_shift(tqs, SLC_BLOCK.bit_length() - 1)
    forced = (blk == 0) | (blk == cur) | (blk == cur - 1)
    causal_b = blk * SLC_BLOCK <= tqs
    blk_f = blk.astype(F32)
    for g in range(G_B):
        imp = jnp.zeros((NSB, TQ), F32)
        for hh in range(HPG_B):
            h = g * HPG_B + hh
            s = jnp.dot(kc_ref[g], q_ref[h], preferred_element_type=F32)
            s = jnp.where(vis, s, NEG)
            m = jnp.max(s, axis=0, keepdims=True)
            e = jnp.where(vis, jnp.exp2(s - m), 0.0)
            p = e / jnp.maximum(jnp.sum(e, axis=0, keepdims=True), 1e-30)
            pc = p.astype(vct_ref.dtype)
            o = jnp.dot(vct_ref[g], pc, preferred_element_type=F32)
            oc_ref[h] = o * g_ref[h:h + 1, :]
            imp = imp + jnp.dot(mt_ref[...], pc, preferred_element_type=F32)
        imp = jnp.where(forced, jnp.inf, imp)
        imp = jnp.where(causal_b, imp, -jnp.inf)
        left = imp
        for _ in range(n_sel):
            m = jnp.max(left, axis=0, keepdims=True)
            idx = jnp.min(jnp.where(left == m, blk_f, float(NSB)), axis=0, keepdims=True)
            left = jnp.where(blk_f == idx, -jnp.inf, left)
        sb_ref[g] = jnp.where(left == -jnp.inf, jnp.where(imp > -jnp.inf, 0.0, NEG), NEG)


def _nsa_cmp(q_t, kc, vc_t, m_t, gates_t, S, n_sel):
    TQ = min(TQ_ATT, S)
    NC = kc.shape[1]
    NSB = m_t.shape[0]
    return pl.pallas_call(
        functools.partial(_nsa_cmp_body, TQ=TQ, NC=NC, NSB=NSB, n_sel=n_sel),
        out_shape=[jax.ShapeDtypeStruct((H_B, HEAD_DIM, S), F32),
                   jax.ShapeDtypeStruct((G_B, NSB, S), F32)],
        grid=(S // TQ,),
        in_specs=[pl.BlockSpec((H_B, HEAD_DIM, TQ), lambda i: (0, 0, i)),
                  pl.BlockSpec((G_B, NC, HEAD_DIM), lambda i: (0, 0, 0)),
                  pl.BlockSpec((G_B, HEAD_DIM, NC), lambda i: (0, 0, 0)),
                  pl.BlockSpec((NSB, NC), lambda i: (0, 0)),
                  pl.BlockSpec((N_GATE, TQ), lambda i: (0, i))],
        out_specs=[pl.BlockSpec((H_B, HEAD_DIM, TQ), lambda i: (0, 0, i)),
                   pl.BlockSpec((G_B, NSB, TQ), lambda i: (0, 0, i))],
        compiler_params=_cp("parallel"),
        name="nsa_cmp_select",
    )(q_t, kc, vc_t, m_t, gates_t)


def _nsa_win_body(q_ref, *rest, TQ, nprev):
    k_refs = rest[:nprev + 1]
    v_refs = rest[nprev + 1:2 * nprev + 2]
    g_ref, o_ref = rest[2 * nprev + 2], rest[2 * nprev + 3]
    qb = pl.program_id(0)
    nk = (nprev + 1) * TQ
    kpos = (qb - nprev) * TQ + lax.broadcasted_iota(jnp.int32, (nk, TQ), 0)
    tq = qb * TQ + lax.broadcasted_iota(jnp.int32, (nk, TQ), 1)
    ok = (kpos >= 0) & (kpos <= tq) & (kpos > tq - WINDOW)
    for g in range(G_B):
        ks = [k_refs[b][:, g * HEAD_DIM:(g + 1) * HEAD_DIM] for b in range(nprev + 1)]
        for hh in range(HPG_B):
            h = g * HPG_B + hh
            q = q_ref[h]
            s = jnp.concatenate([jnp.dot(k, q, preferred_element_type=F32) for k in ks], axis=0)
            s = jnp.where(ok, s, NEG)
            m = jnp.max(s, axis=0, keepdims=True)
            e = jnp.where(ok, jnp.exp2(s - m), 0.0)
            p = (e / jnp.maximum(jnp.sum(e, axis=0, keepdims=True), 1e-30)).astype(v_refs[0].dtype)
            o = jnp.zeros((HEAD_DIM, TQ), F32)
            for b in range(nprev + 1):
                o = o + jnp.dot(v_refs[b][g], p[b * TQ:(b + 1) * TQ, :], preferred_element_type=F32)
            o_ref[h] = o * g_ref[2 * H_B + h:2 * H_B + h + 1, :]


def _nsa_win(q_t, kvb, vw_t, gates_t, S):
    TQ = min(TQ_ATT, S)
    nprev = WINDOW // TQ
    cw = G_B * HEAD_DIM
    kcol = (2 * 2 * G_B * HEAD_DIM) // cw
    k_specs = [pl.BlockSpec((TQ, cw), lambda i, d=d: (jnp.maximum(i - d, 0), kcol)) for d in range(nprev, -1, -1)]
    v_specs = [pl.BlockSpec((G_B, HEAD_DIM, TQ), lambda i, d=d: (0, 0, jnp.maximum(i - d, 0))) for d in range(nprev, -1, -1)]
    return pl.pallas_call(
        functools.partial(_nsa_win_body, TQ=TQ, nprev=nprev),
        out_shape=jax.ShapeDtypeStruct((H_B, HEAD_DIM, S), F32),
        grid=(S // TQ,),
        in_specs=[pl.BlockSpec((H_B, HEAD_DIM, TQ), lambda i: (0, 0, i))] + k_specs + v_specs
                 + [pl.BlockSpec((N_GATE, TQ), lambda i: (0, i))],
        out_specs=pl.BlockSpec((H_B, HEAD_DIM, TQ), lambda i: (0, 0, i)),
        compiler_params=_cp("parallel"),
        name="nsa_window",
    )(q_t, *([kvb] * (nprev + 1)), *([vw_t] * (nprev + 1)), gates_t)


def _nsa_sel_body(q_ref, k_ref, v_ref, sb_ref, g_ref, oc_ref, ow_ref, o_ref, m_ref, acc_ref,
                  s_ref, p_ref, bias_ref, *, TQ, TK, nkt):
    qb, kt = pl.program_id(0), pl.program_id(1)
    last = ((qb + 1) * TQ - 1) // TK
    bpt = TK // SLC_BLOCK

    @pl.when(kt == 0)
    def _():
        _flash_init(m_ref, acc_ref)

    @pl.when(kt <= last)
    def _():
        kpos = kt * TK + lax.broadcasted_iota(jnp.int32, (TK, TQ), 0)
        tq = qb * TQ + lax.broadcasted_iota(jnp.int32, (TK, TQ), 1)
        causal = kpos <= tq
        b0 = pl.multiple_of(kt * bpt, bpt)
        for g in range(G_B):
            bb = sb_ref[g, pl.ds(b0, bpt), :]
            btok = jnp.concatenate(
                [jnp.broadcast_to(bb[r:r + 1, :], (SLC_BLOCK, TQ)) for r in range(bpt)], axis=0)
            bias_ref[g] = jnp.where(causal, btok, NEG)

        def score(h):
            g = h // HPG_B
            k = k_ref[:, g * HEAD_DIM:(g + 1) * HEAD_DIM]
            return jnp.dot(k, q_ref[h], preferred_element_type=F32) + bias_ref[g]

        _flash_heads(H_B, score, lambda h: v_ref[h // HPG_B], m_ref, acc_ref, s_ref, p_ref)

    @pl.when(kt == nkt - 1)
    def _():
        for h in range(H_B):
            o = _flash_out(acc_ref, h) * g_ref[H_B + h:H_B + h + 1, :] + oc_ref[h] + ow_ref[h]
            o_ref[h] = o.astype(o_ref.dtype)


def _nsa_sel(q_t, kvb, vs_t, sel_bias, gates_t, oc_t, ow_t, S):
    TQ, TK = min(TQ_ATT, S), min(TK_ATT, S)
    nqb, nkt = S // TQ, S // TK
    NSB = sel_bias.shape[1]
    cw = G_B * HEAD_DIM
    kcol = (1 * 2 * G_B * HEAD_DIM) // cw

    def kmap(qb, kt):
        return jnp.minimum(kt, ((qb + 1) * TQ - 1) // TK)

    hblk = pl.BlockSpec((H_B, HEAD_DIM, TQ), lambda qb, kt: (0, 0, qb))
    return pl.pallas_call(
        functools.partial(_nsa_sel_body, TQ=TQ, TK=TK, nkt=nkt),
        out_shape=jax.ShapeDtypeStruct((H_B, HEAD_DIM, S), CDT),
        grid=(nqb, nkt),
        in_specs=[hblk,
                  pl.BlockSpec((TK, cw), lambda qb, kt: (kmap(qb, kt), kcol)),
                  pl.BlockSpec((G_B, DV_EXT, TK), lambda qb, kt: (0, 0, kmap(qb, kt))),
                  pl.BlockSpec((G_B, NSB, TQ), lambda qb, kt: (0, 0, qb)),
                  pl.BlockSpec((N_GATE, TQ), lambda qb, kt: (0, qb)),
                  hblk, hblk],
        out_specs=hblk,
        scratch_shapes=_flash_scratch(H_B, TQ, TK) + [pltpu.VMEM((G_B, TK, TQ), F32)],
        compiler_params=_cp("parallel", "arbitrary"),
        name="nsa_select_attn",
    )(q_t, kvb, vs_t, sel_bias, gates_t, oc_t, ow_t)


def _rot_cols(w, half, pad):
    parts = [-w[..., half:2 * half], w[..., :half]]
    if pad:
        parts.append(jnp.zeros(w.shape[:-1] + (pad,), w.dtype))
    return jnp.concatenate(parts, axis=-1)


def _prep_mixer_weights(w_in, w_uq, w_iq, w_uk, w_uv, gate_b):
    D = w_in.shape[0]
    cuts = np.cumsum(IN_SPLITS)[:-1].tolist()
    c_q, c_kv, k_r, k_idx, w_idx, q_b, kv_b, gate, q_m = jnp.split(w_in, cuts, axis=1)
    z = lambda n: jnp.zeros((D, n), w_in.dtype)
    w_pa = jnp.concatenate([
        c_q, c_kv, k_r, k_idx,
        _rot_cols(k_r, D_ROPE // 2, 0), _rot_cols(k_idx, D_IDX_ROPE // 2, D_IDX - D_IDX_ROPE),
        w_idx, z(LANE - H_IDX), gate, z(LANE - N_GATE)], axis=1).astype(CDT)
    w_pq = jnp.concatenate([q_b, q_m], axis=1).astype(CDT)
    w_pkv = kv_b.astype(CDT)
    gb = jnp.concatenate([gate_b, jnp.zeros((LANE - N_GATE,), F32)]).reshape(1, LANE)

    w3 = w_uq.reshape(Q_LORA, H_A, D_NOPE + D_ROPE)
    nope, pe = w3[..., :D_NOPE], w3[..., D_NOPE:]
    zq = jnp.zeros((Q_LORA, H_A, LANE - D_ROPE), w_uq.dtype)
    main = jnp.concatenate([nope, pe, zq], axis=-1).reshape(Q_LORA, H_A * 2 * LANE)
    rot = _rot_cols(pe, D_ROPE // 2, LANE - D_ROPE).reshape(Q_LORA, H_A * LANE)
    w_q = jnp.concatenate([main, rot], axis=1).astype(CDT)

    wi3 = w_iq.reshape(Q_LORA, H_IDX, D_IDX)
    roti = _rot_cols(wi3, D_IDX_ROPE // 2, D_IDX - D_IDX_ROPE).reshape(Q_LORA, H_IDX * D_IDX)
    w_qi = jnp.concatenate([w_iq, roti], axis=1).astype(CDT)

    w_kv = jnp.concatenate([w_uk.reshape(KV_LORA, W_A), w_uv.reshape(KV_LORA, W_A)], axis=1).astype(CDT)
    return w_pa, w_pq, w_pkv, gb, w_q, w_qi, w_kv


def _cmp_to_slc_t(S):
    n_c = S // CMP_STRIDE
    n_sb = S // SLC_BLOCK
    cs = np.arange(n_c) * CMP_STRIDE
    ss = np.arange(n_sb) * SLC_BLOCK
    m = (cs[None, :] < ss[:, None] + SLC_BLOCK) & (cs[None, :] + CMP_LEN > ss[:, None])
    m[:, (S - CMP_LEN) // CMP_STRIDE + 1:] = False
    return jnp.asarray(m.astype(np.float32)).astype(CDT)


def _mixer(x, xc, mem_c, tabs, lw, S):
    (w_pa, w_pq, w_pkv, gb, w_q, w_qi, w_kv, g_cq, g_ckv, cmp_w1, cmp_b1, cmp_w2, cmp_b2, cmp_pe,
     w_mem_kv, w_out) = lw
    rowtab = lambda t: (tabs[t], "row")

    cq, ckv, kk, wi, gates = _mm_rows(
        xc, w_pa, _epi_dsa_prep,
        [(g_cq.reshape(1, -1), "full"), (g_ckv.reshape(1, -1), "full"), (gb, "full"), rowtab(0), rowtab(1)],
        [(Q_LORA, CDT), (KV_LORA, CDT), (LANE, CDT), (LANE, F32), (LANE, F32)], "proj_dsa")
    q_b, q_m = _mm_rows(xc, w_pq, _epi_q_nsa_mem, [rowtab(6), rowtab(7)],
                        [(W_B, CDT), (W_M, CDT)], "proj_q")
    kvb, = _mm_rows(xc, w_pkv, _epi_kv_nsa, [rowtab(6), rowtab(7)], [(N_KVB, CDT)], "proj_kv")

    q_ext, = _mm_rows(cq, w_q, _epi_dsa_q, [rowtab(2), rowtab(3)], [(H_A * 2 * LANE, CDT)], "dsa_q")
    q_idx, = _mm_rows(cq, w_qi, _epi_idx_q, [rowtab(4), rowtab(5)], [(H_IDX * D_IDX, CDT)], "dsa_qi")
    kv_a = _mm(ckv, w_kv, CDT, "dsa_kv")
    k_pe, k_i = kk[:, :D_ROPE], kk[:, D_ROPE:]
    k_nope = kv_a[:, :W_A].reshape(S, H_A, D_NOPE).transpose(1, 0, 2)
    k_ext = jnp.concatenate([
        k_nope, jnp.broadcast_to(k_pe[None], (H_A, S, D_ROPE)),
        jnp.zeros((H_A, S, LANE - D_ROPE), CDT)], axis=-1)
    va_t = kv_a[:, W_A:].reshape(S, H_A, HEAD_DIM).transpose(1, 2, 0)
    qa_t = q_ext.reshape(S, H_A, 2 * LANE).transpose(1, 2, 0)
    nqi = S // TQ_IDX
    qi_t = q_idx.reshape(nqi, TQ_IDX, H_IDX, D_IDX).transpose(0, 3, 2, 1).reshape(nqi, D_IDX, H_IDX * TQ_IDX)
    w_t = wi[:, :H_IDX].reshape(nqi, TQ_IDX, H_IDX).transpose(0, 2, 1)
    topk = min(TOPK_MAX, S // 4)
    dsa_bias = _indexer(qi_t, w_t, k_i, S, topk)
    oa_t = _dsa_attn(qa_t, k_ext, _with_ones_row(va_t), dsa_bias, S)
    o_a = oa_t.transpose(2, 0, 1).reshape(S, W_A)

    gates_t = gates[:, :N_GATE].T
    qb_t = q_b.reshape(S, H_B, HEAD_DIM).transpose(1, 2, 0)
    cw = G_B * HEAD_DIM
    nch = S // CMP_STRIDE
    ld = CMP_LEN * HEAD_DIM

    def cmp_blocks(t):
        ch = t.reshape(nch, CMP_STRIDE, G_B, HEAD_DIM).transpose(2, 0, 1, 3).reshape(G_B, nch, CMP_STRIDE * HEAD_DIM)
        nxt = jnp.concatenate([ch[:, 1:], jnp.zeros((G_B, 1, CMP_STRIDE * HEAD_DIM), ch.dtype)], axis=1)
        return jnp.concatenate([ch, nxt], axis=-1)

    blocks = jnp.stack([cmp_blocks(kvb[:, 0:cw]), cmp_blocks(kvb[:, cw:2 * cw])])
    cmp_out = _compress(blocks, cmp_pe.reshape(2, 1, ld), cmp_w1.astype(CDT), cmp_b1.reshape(2, 1, CMP_HID),
                        cmp_w2.astype(CDT), cmp_b2.reshape(2, 1, HEAD_DIM))
    kc = cmp_out[0]
    vc_t = cmp_out[1].transpose(0, 2, 1)
    n_sb = S // SLC_BLOCK
    oc_t, sel_bias = _nsa_cmp(qb_t, kc, vc_t, _cmp_to_slc_t(S), gates_t, S, min(N_SLC, n_sb))
    vs_t = kvb[:, 3 * cw:4 * cw].reshape(S, G_B, HEAD_DIM).transpose(1, 2, 0)
    vw_t = kvb[:, 5 * cw:6 * cw].reshape(S, G_B, HEAD_DIM).transpose(1, 2, 0)
    ow_t = _nsa_win(qb_t, kvb, vw_t, gates_t, S)
    ob_t = _nsa_sel(qb_t, kvb, _with_ones_row(vs_t), sel_bias, gates_t, oc_t, ow_t, S)
    o_b = ob_t.transpose(2, 0, 1).reshape(S, W_B)

    n_mem = mem_c.shape[0]
    mem_kv = _mm(mem_c, w_mem_kv.astype(CDT), CDT, "mem_kv")
    mk_t = mem_kv[:, :W_M].reshape(n_mem, H_M, D_M).transpose(1, 2, 0)
    mv = mem_kv[:, W_M:].reshape(n_mem, H_M, D_M).transpose(1, 0, 2)
    o_m = _mem_attn(q_m, mk_t, mv)

    mix = _mm(jnp.concatenate([o_a, o_b, o_m], axis=1), w_out.astype(CDT), F32, "mix_out")
    return mix


def kernel(x, mem, positions, w_in, g_cq, g_ckv, w_uq, w_iq, w_uk, w_uv, cmp_w1, cmp_b1, cmp_w2, cmp_b2,
           cmp_pe, gate_b, w_mem_kv, w_out, ln1_g, ln1_b, ln2_g, ln2_b, ffn_wg, ffn_wu, ffn_wd,
           router_w, moe_wg, moe_wu, moe_wd):
    B, S, D = x.shape
    assert B == 1 and S % max(TK_ATT, TK_IDX) == 0 and WINDOW % TQ_ATT == 0
    depth = w_in.shape[0]
    alpha = (2 * depth) ** 0.25
    xs = x[0]
    mem_c = mem[0].astype(CDT)
    tabs = _rope_tables(positions)
    for i in range(depth):
        lw = _prep_mixer_weights(w_in[i], w_uq[i], w_iq[i], w_uk[i], w_uv[i], gate_b[i]) + (
            g_cq[i], g_ckv[i], cmp_w1[i], cmp_b1[i], cmp_w2[i], cmp_b2[i], cmp_pe[i], w_mem_kv[i], w_out[i])
        mix = _mixer(xs, xs.astype(CDT), mem_c, tabs, lw, S)
        xs = _res_ln(xs, mix, ln1_g[i], ln1_b[i], alpha)
        xc = xs.astype(CDT)
        j = i // 2
        if i % 2 == 0:
            h = _swiglu_up(xc, ffn_wg[j].astype(CDT), ffn_wu[j].astype(CDT), "ffn_up")
            f = _mm(h, ffn_wd[j].astype(CDT), F32, "ffn_down")
        else:
            n_exp, _, dfe = moe_wg[j].shape[0], moe_wg[j].shape[1], moe_wg[j].shape[2]
            rw = jnp.concatenate([router_w[j], jnp.zeros((D, LANE - n_exp), F32)], axis=1).astype(CDT)
            gate, = _mm_rows(xc, rw, _make_epi_router(n_exp), [], [(LANE, F32)], "moe_router")
            scale = gate[:, :n_exp].T.reshape(n_exp, S, 1)
            wg = moe_wg[j].transpose(1, 0, 2).reshape(D, n_exp * dfe).astype(CDT)
            wu = moe_wu[j].transpose(1, 0, 2).reshape(D, n_exp * dfe).astype(CDT)
            h = _swiglu_up(xc, wg, wu, "moe_up", scale=scale, n_per_scale=dfe)
            f = _mm(h, moe_wd[j].reshape(n_exp * dfe, D).astype(CDT), F32, "moe_down")
        xs = _res_ln(xs, f, ln2_g[i], ln2_b[i], alpha)
    return xs[None]
```
